```python
import jax, jax.numpy as jnp
from jax import lax
import numpy as np

D_MODEL = 1024
BATCH = 2
SEQ = 16384
DEPTH = 4

CONV_WIDTH = D_MODEL // 2
CONV_K = 31
HGRN_HEAD_DIM = 128
HGRN_HEADS = D_MODEL // HGRN_HEAD_DIM
HGRN_WIDTH = HGRN_HEADS * HGRN_HEAD_DIM
CHUNK = 64
D_FF = 256 * ((8 * D_MODEL // 3 + 255) // 256)
FFN_CONV_K = 3
EPS = 1e-6
MIN_FORGET = 1e-6

IN_WIDTH = 2 * CONV_WIDTH + 5 * HGRN_WIDTH + 2 * D_MODEL
SPLIT_POINTS = (
    2 * CONV_WIDTH,
    2 * CONV_WIDTH + 1 * HGRN_WIDTH,
    2 * CONV_WIDTH + 2 * HGRN_WIDTH,
    2 * CONV_WIDTH + 3 * HGRN_WIDTH,
    2 * CONV_WIDTH + 4 * HGRN_WIDTH,
    2 * CONV_WIDTH + 5 * HGRN_WIDTH,
    2 * CONV_WIDTH + 5 * HGRN_WIDTH + D_MODEL,
)

kernel_name = "bidir_conformer_hgrn2_gated_hybrid"


def _rmsnorm(x, w):
    xf = x.astype(jnp.float32)
    y = xf * lax.rsqrt(jnp.mean(xf * xf, axis=-1, keepdims=True) + EPS)
    return (y * w.astype(jnp.float32)).astype(x.dtype)


def _layernorm(x, w, b):
    xf = x.astype(jnp.float32)
    mu = jnp.mean(xf, axis=-1, keepdims=True)
    var = jnp.mean(jnp.square(xf - mu), axis=-1, keepdims=True)
    y = (xf - mu) * lax.rsqrt(var + EPS)
    return (y * w.astype(jnp.float32) + b.astype(jnp.float32)).astype(x.dtype)


def _dwconv_centred(x, w, b):
    k = w.shape[0]
    pad = k // 2
    y = lax.conv_general_dilated(
        x, w[:, None, :].astype(x.dtype), window_strides=(1,), padding=[(pad, pad)],
        dimension_numbers=("NWC", "WIO", "NWC"), feature_group_count=x.shape[-1])
    return y + b.astype(x.dtype)


def _chunk_gated_scan(q, k, v, lf):
    b_, s_, h_, dk = q.shape
    dv = v.shape[-1]
    nc = s_ // CHUNK

    def to_chunks(t):
        return t.astype(jnp.float32).reshape(b_, nc, CHUNK, h_, t.shape[-1]).transpose(1, 0, 3, 2, 4)

    qc, kc, vc, lfc = to_chunks(q), to_chunks(k), to_chunks(v), to_chunks(lf)
    pos = jnp.arange(CHUNK)
    tril = (pos[:, None] >= pos[None, :])[:, :, None]

    def step(state, inp):
        qb, kb, vb, lfb = inp
        a = jnp.cumsum(lfb, axis=2)
        diff = a[:, :, :, None, :] - a[:, :, None, :, :]
        decay = jnp.where(tril, jnp.exp(jnp.minimum(diff, 0.0)), 0.0)
        scores = jnp.einsum("bhtk,bhsk,bhtsk->bhts", qb, kb, decay)
        o = (jnp.einsum("bhts,bhsv->bhtv", scores, vb)
             + jnp.einsum("bhtk,bhkv->bhtv", qb * jnp.exp(a), state))
        a_last = a[:, :, -1:, :]
        new_state = (jnp.exp(a_last[:, :, 0, :])[..., None] * state
                     + jnp.einsum("bhsk,bhsv->bhkv", kb * jnp.exp(a_last - a), vb))
        return new_state, o

    s0 = jnp.zeros((b_, h_, dk, dv), jnp.float32)
    _, o = lax.scan(step, s0, (qc, kc, vc, lfc))
    return o.transpose(1, 0, 3, 2, 4).reshape(b_, s_, h_, dv)


def _forget_log_and_key(z, lb):
    zf = z.astype(jnp.float32)
    f = lb + (1.0 - lb) * jax.nn.sigmoid(zf)
    lf = jnp.log(jnp.clip(f, MIN_FORGET, 1.0))
    key_gate = (1.0 - lb) * jax.nn.sigmoid(-zf)
    return lf, key_gate


def _bi_hgrn2(q, v, z_fwd, z_bwd, lb_fwd, lb_bwd):
    b_, s_, _ = q.shape
    heads = lambda t: t.reshape(b_, s_, HGRN_HEADS, HGRN_HEAD_DIM)
    qh = heads(jax.nn.silu(q.astype(jnp.float32)) * (HGRN_HEAD_DIM ** -0.5))
    vh = heads(v)
    lf_f, k_f = _forget_log_and_key(z_fwd, lb_fwd)
    lf_b, k_b = _forget_log_and_key(z_bwd, lb_bwd)
    o_fwd = _chunk_gated_scan(qh, heads(k_f), vh, heads(lf_f))
    flip = lambda t: jnp.flip(t, axis=1)
    o_bwd = flip(_chunk_gated_scan(flip(qh), flip(heads(k_b)), flip(vh), flip(heads(lf_b))))
    return o_fwd + o_bwd


def _token_mixers(h, w_in, dw_w, dw_b, ln_w, ln_b, pw_w, lb, norm_w, o_w):
    proj = jnp.einsum("bsd,de->bse", h, w_in)
    glu_in, q, v, z_f, z_b, og, g_a, g_b = jnp.split(proj, SPLIT_POINTS, axis=-1)
    a = jax.nn.glu(glu_in, axis=-1)
    a = _dwconv_centred(a, dw_w, dw_b)
    a = jax.nn.silu(_layernorm(a, ln_w, ln_b))
    a = jnp.einsum("bsc,cd->bsd", a, pw_w)
    o = _bi_hgrn2(q, v, z_f, z_b, lb[0], lb[1])
    of = o * lax.rsqrt(jnp.mean(o * o, axis=-1, keepdims=True) + EPS)
    of = of.reshape(o.shape[0], o.shape[1], HGRN_WIDTH) * norm_w.astype(jnp.float32)
    bmix = (of * jax.nn.silu(og.astype(jnp.float32))).astype(h.dtype)
    bmix = jnp.einsum("bse,ed->bsd", bmix, o_w)
    return jax.nn.sigmoid(g_a) * a + jax.nn.sigmoid(g_b) * bmix


def _conv_glu_ffn(h, w_up, dw_w, dw_b, w_down):
    up = jnp.einsum("bsd,df->bsf", h, w_up)
    gate, val = jnp.split(up, 2, axis=-1)
    gate = _dwconv_centred(gate, dw_w, dw_b)
    return jnp.einsum("bsf,fd->bsd", jax.nn.silu(gate) * val, w_down)


def setup_inputs(seed: int = 0) -> dict:
    key = jax.random.key(seed)
    ks = jax.random.split(key, 18)

    def nrm(k, shape, scale):
        return scale * jax.random.normal(k, shape, jnp.float32)

    def gain(k, shape):
        return 1.0 + 0.02 * jax.random.normal(k, shape, jnp.float32)

    return {
        "x": nrm(ks[0], (BATCH, SEQ, D_MODEL), 1.0),
        "attn_norm_w": gain(ks[1], (DEPTH, D_MODEL)),
        "w_in": nrm(ks[2], (DEPTH, D_MODEL, IN_WIDTH), D_MODEL ** -0.5),
        "conv_dw_w": nrm(ks[3], (DEPTH, CONV_K, CONV_WIDTH), CONV_K ** -0.5),
        "conv_dw_b": nrm(ks[4], (DEPTH, CONV_WIDTH), 0.02),
        "conv_ln_w": gain(ks[5], (DEPTH, CONV_WIDTH)),
        "conv_ln_b": nrm(ks[6], (DEPTH, CONV_WIDTH), 0.02),
        "conv_pw_w": nrm(ks[7], (DEPTH, CONV_WIDTH, D_MODEL), CONV_WIDTH ** -0.5),
        "lb_logits": nrm(ks[8], (DEPTH, 2, HGRN_WIDTH), 0.5),
        "hgrn_norm_w": gain(ks[9], (DEPTH, HGRN_WIDTH)),
        "hgrn_o_w": nrm(ks[10], (DEPTH, HGRN_WIDTH, D_MODEL), HGRN_WIDTH ** -0.5),
        "w_out": nrm(ks[11], (DEPTH, D_MODEL, D_MODEL), D_MODEL ** -0.5),
        "ffn_norm_w": gain(ks[12], (DEPTH, D_MODEL)),
        "ffn_w_up": nrm(ks[13], (DEPTH, D_MODEL, 2 * D_FF), D_MODEL ** -0.5),
        "ffn_dw_w": nrm(ks[14], (DEPTH, FFN_CONV_K, D_FF), FFN_CONV_K ** -0.5),
        "ffn_dw_b": nrm(ks[15], (DEPTH, D_FF), 0.02),
        "ffn_w_down": nrm(ks[16], (DEPTH, D_FF, D_MODEL), D_FF ** -0.5),
        "final_norm_w": gain(ks[17], (D_MODEL,)),
    }


def reference(x, attn_norm_w, w_in, conv_dw_w, conv_dw_b, conv_ln_w, conv_ln_b, conv_pw_w,
              lb_logits, hgrn_norm_w, hgrn_o_w, w_out, ffn_norm_w, ffn_w_up, ffn_dw_w,
              ffn_dw_b, ffn_w_down, final_norm_w):
    p = jax.nn.softmax(lb_logits.astype(jnp.float32), axis=0)
    lower_bounds = jnp.cumsum(p, axis=0) - p[0:1]
    for layer in range(DEPTH):
        h = _rmsnorm(x, attn_norm_w[layer])
        y = _token_mixers(h, w_in[layer], conv_dw_w[layer], conv_dw_b[layer], conv_ln_w[layer],
                          conv_ln_b[layer], conv_pw_w[layer], lower_bounds[layer],
                          hgrn_norm_w[layer], hgrn_o_w[layer])
        x = x + jnp.einsum("bsd,de->bse", y, w_out[layer])
        h2 = _rmsnorm(x, ffn_norm_w[layer])
        x = x + _conv_glu_ffn(h2, ffn_w_up[layer], ffn_dw_w[layer], ffn_dw_b[layer], ffn_w_down[layer])
    return _rmsnorm(x, final_norm_w)
```

```python
import functools

import numpy as np
import jax
import jax.numpy as jnp
from jax import lax
from jax.experimental import pallas as pl
from jax.experimental.pallas import tpu as pltpu

F32 = jnp.float32
BF16 = jnp.bfloat16

EPS = 1e-6
MIN_FORGET = 1e-6
HEAD_DIM = 128
CONV_K = 31
FFN_CONV_K = 3
CONV_HALO = 16
FFN_HALO = 8
SCAN_BLOCK = 128
SCAN_CHUNK = 256
TOKEN_TILE = 512
VMEM_LIMIT = 56 * 1024 * 1024


def _dot(a, b):
    return jnp.dot(a, b, preferred_element_type=F32)


def _dot_nt(a, b):
    return lax.dot_general(a, b, (((1,), (1,)), ((), ())), preferred_element_type=F32)


def _dot_tn(a, b):
    return lax.dot_general(a, b, (((0,), (0,)), ((), ())), preferred_element_type=F32)


def _sigmoid(x):
    return 1.0 / (1.0 + jnp.exp(-x))


def _silu(x):
    return x * _sigmoid(x)


def _resident(shape):
    nd = len(shape)
    return pl.BlockSpec(shape, lambda *_: (0,) * nd, pipeline_mode=pl.Buffered(1))


def _in_proj_kernel(layer, depth, x_ref, nw_ref, w_ref, lbl_ref,
                    glu_ref, q_ref, v_ref, lff_ref, kf_ref, lfb_ref, kb_ref, og_ref, ga_ref, gb_ref):
    d = x_ref.shape[1]
    x = x_ref[...]
    h = x * lax.rsqrt(jnp.mean(x * x, axis=-1, keepdims=True) + EPS) * nw_ref[...]
    hb = h.astype(BF16)

    def sec(lo, width):
        return _dot(hb, w_ref[:, lo:lo + width])

    half = d // 2
    glu_ref[...] = sec(0, half) * _sigmoid(sec(half, half))
    q = sec(d, d)
    q_ref[...] = (_silu(q) * (HEAD_DIM ** -0.5)).astype(q_ref.dtype)
    v_ref[...] = sec(2 * d, d).astype(v_ref.dtype)

    def lower_bound(direction):
        rows = [lbl_ref[2 * i + direction:2 * i + direction + 1, :] for i in range(depth)]
        m = functools.reduce(jnp.maximum, rows)
        e = [jnp.exp(r - m) for r in rows]
        den = functools.reduce(lambda a, b: a + b, e)
        p = [ei / den for ei in e]
        cum = functools.reduce(lambda a, b: a + b, p[:layer + 1])
        return cum - p[0]

    def forget(z, lb, lf_ref, k_ref):
        s = _sigmoid(z)
        f = lb + (1.0 - lb) * s
        lf_ref[...] = jnp.log(jnp.clip(f, MIN_FORGET, 1.0))
        k_ref[...] = ((1.0 - lb) * _sigmoid(-z)).astype(k_ref.dtype)

    forget(sec(3 * d, d), lower_bound(0), lff_ref, kf_ref)
    forget(sec(4 * d, d), lower_bound(1), lfb_ref, kb_ref)
    og_ref[...] = _silu(sec(5 * d, d)).astype(og_ref.dtype)
    ga_ref[...] = _sigmoid(sec(6 * d, d)).astype(ga_ref.dtype)
    gb_ref[...] = _sigmoid(sec(7 * d, d)).astype(gb_ref.dtype)


def _in_proj(layer, depth, x2, norm_w, w_in_b, lbl):
    t, d = x2.shape
    tm = TOKEN_TILE
    row = lambda width: pl.BlockSpec((tm, width), lambda i: (i, 0))
    act = lambda dtype, width=d: jax.ShapeDtypeStruct((t, width), dtype)
    return pl.pallas_call(
        functools.partial(_in_proj_kernel, layer, depth),
        grid=(t // tm,),
        in_specs=[row(d), _resident((1, d)), _resident(w_in_b.shape), _resident(lbl.shape)],
        out_specs=[row(d // 2)] + [row(d)] * 9,
        out_shape=[act(F32, d // 2), act(BF16), act(BF16), act(F32), act(BF16), act(F32), act(BF16),
                   act(BF16), act(BF16), act(BF16)],
        compiler_params=pltpu.CompilerParams(dimension_semantics=("arbitrary",),
                                             vmem_limit_bytes=VMEM_LIMIT),
        name="in_proj",
    )(x2, norm_w, w_in_b, lbl)


def _scan_tables(chunk, reverse):
    idx = np.arange(chunk)
    tri = (idx[None, :] >= idx[:, None]) if reverse else (idx[None, :] <= idx[:, None])
    b = np.arange(SCAN_BLOCK)
    xor = b[:, None] ^ b[None, :]
    level = np.floor(np.log2(np.maximum(xor, 1))).astype(np.int32)
    before = (b[None, :] > b[:, None]) if reverse else (b[None, :] < b[:, None])
    code = np.where(xor == 0, -1, np.where(before, level, -2)).astype(np.int32)
    return jnp.asarray(tri, BF16), jnp.asarray(code)


def _scan_kernel(reverse, lf_ref, q_ref, k_ref, v_ref, tri_ref, code_ref, o_ref, st_ref, c_ref):
    chunk, width = lf_ref.shape
    nblk = chunk // SCAN_BLOCK
    nlevel = SCAN_BLOCK.bit_length() - 1
    heads = width // HEAD_DIM

    @pl.when(pl.program_id(1) == 0)
    def _():
        st_ref[...] = jnp.zeros_like(st_ref)

    lf = lf_ref[...]
    hi = lf.astype(BF16)
    r1 = lf - hi.astype(F32)
    mid = r1.astype(BF16)
    lo = (r1 - mid.astype(F32)).astype(BF16)
    tri = tri_ref[...]
    c_ref[...] = _dot(tri, hi) + _dot(tri, mid) + _dot(tri, lo)

    code = code_ref[...]
    row = lax.broadcasted_iota(jnp.int32, (chunk, HEAD_DIM), 0)
    last = 0 if reverse else chunk - 1
    blocks = [slice(i * SCAN_BLOCK, (i + 1) * SCAN_BLOCK) for i in range(nblk)]

    def head(h, carry):
        ls = pl.ds(pl.multiple_of(h * HEAD_DIM, HEAD_DIM), HEAD_DIM)
        c = c_ref[:, ls]
        qb = q_ref[:, ls]
        kb = k_ref[:, ls]
        vb = v_ref[:, ls]
        q = qb.astype(F32)
        k = kb.astype(F32)

        def ref_rows(level):
            b = 1 << level
            if level == 0:
                return jnp.where((row & 1) == 1, c, pltpu.roll(c, chunk - 1, 0))
            if level == 1:
                m = row & 3
                return jnp.where(m == 0, pltpu.roll(c, chunk - 2, 0),
                                 jnp.where(m == 1, pltpu.roll(c, chunk - 1, 0),
                                           jnp.where(m == 2, c, pltpu.roll(c, 1, 0))))
            parts = [jnp.broadcast_to(c_ref[pl.ds((2 * j + 1) * b, 1), ls], (2 * b, HEAD_DIM))
                     for j in range(chunk // (2 * b))]
            return parts[0] if len(parts) == 1 else jnp.concatenate(parts, axis=0)

        def scaled(level):
            e = jnp.exp(-jnp.abs(c - ref_rows(level)))
            return (q * e).astype(BF16), (k * e).astype(BF16)

        scores = [jnp.where(code == -1, _dot_nt(qb[blk], kb[blk]), 0.0) for blk in blocks]
        for level in range(nlevel):
            qs, ks = scaled(level)
            scores = [jnp.where(code == level, _dot_nt(qs[blk], ks[blk]), sc)
                      for blk, sc in zip(blocks, scores)]
        out = [_dot(sc.astype(BF16), vb[blk]) for blk, sc in zip(blocks, scores)]
        if nblk == 2:
            qs, ks = scaled(nlevel)
            late, early = (0, 1) if reverse else (1, 0)
            cross = _dot_nt(qs[blocks[late]], ks[blocks[early]])
            out[late] = out[late] + _dot(cross.astype(BF16), vb[blocks[early]])
        o_intra = out[0] if nblk == 1 else jnp.concatenate(out, axis=0)

        st = st_ref[h]
        o_inter = _dot_nt((q * jnp.exp(c)).astype(BF16), st.astype(BF16))
        o_ref[:, ls] = o_intra + o_inter

        c_last = c_ref[pl.ds(last, 1), ls]
        ke = (k * jnp.exp(c_last - c)).astype(BF16)
        st_ref[h] = st * jnp.exp(c_last) + _dot_tn(vb, ke)
        return carry

    lax.fori_loop(0, heads, head, 0)


def _scan(reverse, batch, lf, q, k, v):
    t, width = lf.shape
    chunk = SCAN_CHUNK
    nc = t // batch // chunk
    tri, code = _scan_tables(chunk, reverse)
    if reverse:
        idx = lambda b, i: (b * nc + nc - 1 - i, 0)
    else:
        idx = lambda b, i: (b * nc + i, 0)
    row = pl.BlockSpec((chunk, width), idx)
    heads = width // HEAD_DIM
    return pl.pallas_call(
        functools.partial(_scan_kernel, reverse),
        grid=(batch, nc),
        in_specs=[row, row, row, row, _resident(tri.shape), _resident(code.shape)],
        out_specs=row,
        out_shape=jax.ShapeDtypeStruct((t, width), F32),
        scratch_shapes=[pltpu.VMEM((heads, HEAD_DIM, HEAD_DIM), F32), pltpu.VMEM((chunk, width), F32)],
        compiler_params=pltpu.CompilerParams(dimension_semantics=("arbitrary", "arbitrary"),
                                             vmem_limit_bytes=VMEM_LIMIT),
        name="scan_bwd" if reverse else "scan_fwd",
    )(lf, q, k, v, tri, code)


def _mix_out_kernel(tiles_per_seq, x_ref, glu_ref, glu_prev_ref, glu_next_ref, of_ref, ob_ref, og_ref,
                    ga_ref, gb_ref, dww_ref, dwb_ref, lnw_ref, lnb_ref, pw_ref, hnw_ref, ow_ref, wout_ref,
                    xo_ref, ext_ref):
    tm, cw = glu_ref.shape
    i = pl.program_id(0)
    first = (i % tiles_per_seq) == 0
    final = (i % tiles_per_seq) == tiles_per_seq - 1

    ext_ref[pl.ds(0, CONV_HALO), :] = jnp.where(first, 0.0, glu_prev_ref[...])
    ext_ref[pl.ds(CONV_HALO, tm), :] = glu_ref[...]
    ext_ref[pl.ds(CONV_HALO + tm, CONV_HALO), :] = jnp.where(final, 0.0, glu_next_ref[...])
    base = CONV_HALO - CONV_K // 2
    acc = jnp.broadcast_to(dwb_ref[...], (tm, cw))
    for j in range(CONV_K):
        acc = acc + dww_ref[pl.ds(j, 1), :] * ext_ref[pl.ds(base + j, tm), :]
    mu = jnp.mean(acc, axis=-1, keepdims=True)
    cen = acc - mu
    var = jnp.mean(cen * cen, axis=-1, keepdims=True)
    a = _silu(cen * lax.rsqrt(var + EPS) * lnw_ref[...] + lnb_ref[...])
    a = _dot(a.astype(BF16), pw_ref[...])

    o = of_ref[...] + ob_ref[...]
    parts = []
    for h in range(o.shape[1] // HEAD_DIM):
        oh = o[:, h * HEAD_DIM:(h + 1) * HEAD_DIM]
        parts.append(oh * lax.rsqrt(jnp.mean(oh * oh, axis=-1, keepdims=True) + EPS))
    of = jnp.concatenate(parts, axis=1) * hnw_ref[...]
    bmix = _dot((of * og_ref[...].astype(F32)).astype(BF16), ow_ref[...])

    y = ga_ref[...].astype(F32) * a + gb_ref[...].astype(F32) * bmix
    xo_ref[...] = x_ref[...] + _dot(y.astype(BF16), wout_ref[...])


def _mix_out(seq, x2, glu, o_f, o_b, og, ga, gb, dw_w, dw_b, ln_w, ln_b, pw_b, hn_w, ow_b, wout_b):
    t, d = x2.shape
    cw = glu.shape[1]
    tm = TOKEN_TILE
    hb = tm // CONV_HALO
    nhalo = t // CONV_HALO
    row = lambda width: pl.BlockSpec((tm, width), lambda i: (i, 0))
    prev = pl.BlockSpec((CONV_HALO, cw), lambda i: (jnp.maximum(i * hb - 1, 0), 0))
    nxt = pl.BlockSpec((CONV_HALO, cw), lambda i: (jnp.minimum((i + 1) * hb, nhalo - 1), 0))
    weights = (dw_w, dw_b, ln_w, ln_b, pw_b, hn_w, ow_b, wout_b)
    return pl.pallas_call(
        functools.partial(_mix_out_kernel, seq // tm),
        grid=(t // tm,),
        in_specs=[row(d), row(cw), prev, nxt] + [row(d)] * 5 + [_resident(w.shape) for w in weights],
        out_specs=row(d),
        out_shape=jax.ShapeDtypeStruct((t, d), F32),
        scratch_shapes=[pltpu.VMEM((tm + 2 * CONV_HALO, cw), F32)],
        compiler_params=pltpu.CompilerParams(dimension_semantics=("arbitrary",),
                                             vmem_limit_bytes=VMEM_LIMIT),
        name="mix_out",
    )(x2, glu, glu, glu, o_f, o_b, og, ga, gb, *weights)


def _ffn_kernel(tiles_per_seq, ff_chunks, apply_final, x_ref, x_prev_ref, x_next_ref, nw_ref, wup_ref,
                dww_ref, dwb_ref, wdn_ref, fnw_ref, xo_ref, hext_ref, gext_ref):
    tm, d = x_ref.shape
    dff = wdn_ref.shape[0]
    fc = dff // ff_chunks
    i = pl.program_id(0)
    first = (i % tiles_per_seq) == 0
    final = (i % tiles_per_seq) == tiles_per_seq - 1

    def norm(x):
        return (x * lax.rsqrt(jnp.mean(x * x, axis=-1, keepdims=True) + EPS) * nw_ref[...]).astype(BF16)

    x = x_ref[...]
    hext_ref[pl.ds(0, FFN_HALO), :] = norm(x_prev_ref[...])
    hext_ref[pl.ds(FFN_HALO, tm), :] = norm(x)
    hext_ref[pl.ds(FFN_HALO + tm, FFN_HALO), :] = norm(x_next_ref[...])

    rows = lax.broadcasted_iota(jnp.int32, (tm + 2 * FFN_HALO, 1), 0)
    outside = (first & (rows < FFN_HALO)) | (final & (rows >= FFN_HALO + tm))
    base = FFN_HALO - FFN_CONV_K // 2
    acc = x
    for cidx in range(ff_chunks):
        lo = cidx * fc
        gate = _dot(hext_ref[...], wup_ref[:, lo:lo + fc])
        gext_ref[...] = jnp.where(outside, 0.0, gate)
        conv = jnp.broadcast_to(dwb_ref[:, lo:lo + fc], (tm, fc))
        for j in range(FFN_CONV_K):
            conv = conv + dww_ref[pl.ds(j, 1), lo:lo + fc] * gext_ref[pl.ds(base + j, tm), :]
        val = _dot(hext_ref[pl.ds(FFN_HALO, tm), :], wup_ref[:, dff + lo:dff + lo + fc])
        acc = acc + _dot((_silu(conv) * val).astype(BF16), wdn_ref[lo:lo + fc, :])
    if apply_final:
        acc = acc * lax.rsqrt(jnp.mean(acc * acc, axis=-1, keepdims=True) + EPS) * fnw_ref[...]
    xo_ref[...] = acc


def _ffn(seq, apply_final, x2, norm_w, wup_b, dw_w, dw_b, wdn_b, final_w):
    t, d = x2.shape
    dff = wdn_b.shape[0]
    tm = TOKEN_TILE
    ff_chunks = 2
    hb = tm // FFN_HALO
    nhalo = t // FFN_HALO
    row = pl.BlockSpec((tm, d), lambda i: (i, 0))
    prev = pl.BlockSpec((FFN_HALO, d), lambda i: (jnp.maximum(i * hb - 1, 0), 0))
    nxt = pl.BlockSpec((FFN_HALO, d), lambda i: (jnp.minimum((i + 1) * hb, nhalo - 1), 0))
    weights = (norm_w, wup_b, dw_w, dw_b, wdn_b, final_w)
    return pl.pallas_call(
        functools.partial(_ffn_kernel, seq // tm, ff_chunks, apply_final),
        grid=(t // tm,),
        in_specs=[row, prev, nxt] + [_resident(w.shape) for w in weights],
        out_specs=row,
        out_shape=jax.ShapeDtypeStruct((t, d), F32),
        scratch_shapes=[pltpu.VMEM((tm + 2 * FFN_HALO, d), BF16),
                        pltpu.VMEM((tm + 2 * FFN_HALO, dff // ff_chunks), F32)],
        compiler_params=pltpu.CompilerParams(dimension_semantics=("arbitrary",),
                                             vmem_limit_bytes=VMEM_LIMIT),
        name="ffn",
    )(x2, x2, x2, *weights)


def kernel(x, attn_norm_w, w_in, conv_dw_w, conv_dw_b, conv_ln_w, conv_ln_b, conv_pw_w, lb_logits,
           hgrn_norm_w, hgrn_o_w, w_out, ffn_norm_w, ffn_w_up, ffn_dw_w, ffn_dw_b, ffn_w_down, final_norm_w):
    batch, seq, d = x.shape
    depth = w_in.shape[0]
    x2 = x.reshape(batch * seq, d)
    lbl = lb_logits.astype(F32).reshape(depth * 2, -1)
    vec = lambda w: w.reshape(1, -1).astype(F32)
    for layer in range(depth):
        glu, q, v, lf_f, k_f, lf_b, k_b, og, ga, gb = _in_proj(
            layer, depth, x2, vec(attn_norm_w[layer]), w_in[layer].astype(BF16), lbl)
        o_f = _scan(False, batch, lf_f, q, k_f, v)
        o_b = _scan(True, batch, lf_b, q, k_b, v)
        x2 = _mix_out(seq, x2, glu, o_f, o_b, og, ga, gb,
                      conv_dw_w[layer], vec(conv_dw_b[layer]), vec(conv_ln_w[layer]), vec(conv_ln_b[layer]),
                      conv_pw_w[layer].astype(BF16), vec(hgrn_norm_w[layer]),
                      hgrn_o_w[layer].astype(BF16), w_out[layer].astype(BF16))
        x2 = _ffn(seq, layer == depth - 1, x2, vec(ffn_norm_w[layer]), ffn_w_up[layer].astype(BF16),
                  ffn_dw_w[layer], vec(ffn_dw_b[layer]), ffn_w_down[layer].astype(BF16), vec(final_norm_w))
    return x2.reshape(batch, seq, d)
```

```python
import functools

import numpy as np
import jax
import jax.numpy as jnp
from jax import lax
from jax.experimental import pallas as pl
from jax.experimental.pallas import tpu as pltpu

F32 = jnp.float32
BF16 = jnp.bfloat16

EPS = 1e-6
MIN_FORGET = 1e-6
HEAD_DIM = 128
CONV_K = 31
FFN_CONV_K = 3
CONV_HALO = 16
FFN_HALO = 8
SCAN_BLOCK = 128
SCAN_CHUNK = 256
SCAN_HEAD_GROUP = 8
TOKEN_TILE = 512
VMEM_LIMIT = 56 * 1024 * 1024


def _dot(a, b):
    return jnp.dot(a, b, preferred_element_type=F32)


def _dot_nt(a, b):
    return lax.dot_general(a, b, (((1,), (1,)), ((), ())), preferred_element_type=F32)


def _dot_tn(a, b):
    return lax.dot_general(a, b, (((0,), (0,)), ((), ())), preferred_element_type=F32)


def _sigmoid(x):
    return 1.0 / (1.0 + jnp.exp(-x))


def _silu(x):
    return x * _sigmoid(x)


def _resident(shape):
    nd = len(shape)
    return pl.BlockSpec(shape, lambda *_: (0,) * nd, pipeline_mode=pl.Buffered(1))


def _in_proj_kernel(layer, depth, x_ref, nw_ref, w_ref, lbl_ref,
                    glu_ref, q_ref, v_ref, lff_ref, kf_ref, lfb_ref, kb_ref, og_ref, ga_ref, gb_ref):
    d = x_ref.shape[1]
    x = x_ref[...]
    h = x * lax.rsqrt(jnp.mean(x * x, axis=-1, keepdims=True) + EPS) * nw_ref[...]
    hb = h.astype(BF16)

    def sec(lo, width):
        return _dot(hb, w_ref[:, lo:lo + width])

    half = d // 2
    glu_ref[...] = sec(0, half) * _sigmoid(sec(half, half))
    q = sec(d, d)
    q_ref[...] = (_silu(q) * (HEAD_DIM ** -0.5)).astype(q_ref.dtype)
    v_ref[...] = sec(2 * d, d).astype(v_ref.dtype)

    def lower_bound(direction):
        rows = [lbl_ref[2 * i + direction:2 * i + direction + 1, :] for i in range(depth)]
        m = functools.reduce(jnp.maximum, rows)
        e = [jnp.exp(r - m) for r in rows]
        den = functools.reduce(lambda a, b: a + b, e)
        p = [ei / den for ei in e]
        cum = functools.reduce(lambda a, b: a + b, p[:layer + 1])
        return cum - p[0]

    def forget(z, lb, lf_ref, k_ref):
        s = _sigmoid(z)
        f = lb + (1.0 - lb) * s
        lf_ref[...] = jnp.log2(jnp.clip(f, MIN_FORGET, 1.0))
        k_ref[...] = ((1.0 - lb) * _sigmoid(-z)).astype(k_ref.dtype)

    forget(sec(3 * d, d), lower_bound(0), lff_ref, kf_ref)
    forget(sec(4 * d, d), lower_bound(1), lfb_ref, kb_ref)
    og_ref[...] = _silu(sec(5 * d, d)).astype(og_ref.dtype)
    ga_ref[...] = _sigmoid(sec(6 * d, d)).astype(ga_ref.dtype)
    gb_ref[...] = _sigmoid(sec(7 * d, d)).astype(gb_ref.dtype)


def _in_proj(layer, depth, x2, norm_w, w_in_b, lbl):
    t, d = x2.shape
    tm = TOKEN_TILE
    row = lambda width: pl.BlockSpec((tm, width), lambda i: (i, 0))
    act = lambda dtype, width=d: jax.ShapeDtypeStruct((t, width), dtype)
    return pl.pallas_call(
        functools.partial(_in_proj_kernel, layer, depth),
        grid=(t // tm,),
        in_specs=[row(d), _resident((1, d)), _resident(w_in_b.shape), _resident(lbl.shape)],
        out_specs=[row(d // 2)] + [row(d)] * 9,
        out_shape=[act(F32, d // 2), act(BF16), act(BF16), act(F32), act(BF16), act(F32), act(BF16),
                   act(BF16), act(BF16), act(BF16)],
        compiler_params=pltpu.CompilerParams(dimension_semantics=("arbitrary",),
                                             vmem_limit_bytes=VMEM_LIMIT),
        name="in_proj",
    )(x2, norm_w, w_in_b, lbl)


def _scan_tables(chunk, reverse):
    idx = np.arange(chunk)
    tri = (idx[None, :] >= idx[:, None]) if reverse else (idx[None, :] <= idx[:, None])
    b = np.arange(SCAN_BLOCK)
    xor = b[:, None] ^ b[None, :]
    level = np.floor(np.log2(np.maximum(xor, 1))).astype(np.int32)
    before = (b[None, :] > b[:, None]) if reverse else (b[None, :] < b[:, None])
    code = np.where(xor == 0, -1, np.where(before, level, -2)).astype(np.int32)
    return jnp.asarray(tri, BF16), jnp.asarray(code)


def _scan_kernel(reverse, group, lf_ref, q_ref, k_ref, v_ref, tri_ref, code_ref, o_ref, st_ref, c_ref):
    chunk, width = lf_ref.shape
    nblk = chunk // SCAN_BLOCK
    nlevel = SCAN_BLOCK.bit_length() - 1
    heads = width // HEAD_DIM
    sub = 8

    @pl.when(pl.program_id(1) == 0)
    def _():
        st_ref[...] = jnp.zeros_like(st_ref)

    tri = tri_ref[...]
    code = code_ref[...]
    sublane = lax.broadcasted_iota(jnp.int32, (sub, HEAD_DIM), 0)
    last = 0 if reverse else chunk - 1
    blocks = [slice(i * SCAN_BLOCK, (i + 1) * SCAN_BLOCK) for i in range(nblk)]

    def neg_abs(x):
        return lax.bitcast_convert_type(lax.bitcast_convert_type(x, jnp.int32) | jnp.int32(-2 ** 31), F32)

    def head(h):
        ls = pl.ds(pl.multiple_of(h * HEAD_DIM, HEAD_DIM), HEAD_DIM)
        c = c_ref[:, ls]
        qb = q_ref[:, ls]
        kb = k_ref[:, ls]
        vb = v_ref[:, ls]

        def ref_row(r, rows):
            return jnp.broadcast_to(c_ref[pl.ds(r, 1), ls], (rows, HEAD_DIM))

        def exponent(level):
            b = 1 << level
            if level == 0:
                lfh = lf_ref[:, ls]
                nxt = lfh if reverse else pltpu.roll(lfh, chunk - 1, 0)
                return jnp.concatenate(
                    [jnp.where((sublane & 1) == 0, nxt[g * sub:(g + 1) * sub], 0.0) for g in range(chunk // sub)], 0)
            if b < sub:
                parts = []
                for g in range(chunk // sub):
                    rows = [ref_row(g * sub + (2 * j + 1) * b, sub) for j in range(sub // (2 * b))]
                    ref = rows[0]
                    for j in range(1, len(rows)):
                        ref = jnp.where(sublane < 2 * b * j, ref, rows[j])
                    parts.append(neg_abs(c[g * sub:(g + 1) * sub] - ref))
                return jnp.concatenate(parts, 0)
            parts = []
            for j in range(chunk // (2 * b)):
                ref = ref_row((2 * j + 1) * b, b)
                even = c[2 * j * b:(2 * j + 1) * b]
                odd = c[(2 * j + 1) * b:(2 * j + 2) * b]
                parts += [even - ref, ref - odd] if reverse else [ref - even, odd - ref]
            return jnp.concatenate(parts, 0)

        qf = qb.astype(F32)
        kf = kb.astype(F32)

        def scaled(level):
            b = 1 << level
            later, earlier = (kf, qf) if reverse else (qf, kf)
            if b < sub:
                odd = (sublane & b) != 0
                mixed = jnp.concatenate(
                    [jnp.where(odd, later[g * sub:(g + 1) * sub], earlier[g * sub:(g + 1) * sub])
                     for g in range(chunk // sub)], 0)
            else:
                mixed = jnp.concatenate(
                    [(later if j % 2 else earlier)[j * b:(j + 1) * b] for j in range(chunk // b)], 0)
            return (mixed * jnp.exp2(exponent(level))).astype(BF16)

        scores = [jnp.where(code == -1, _dot_nt(qb[blk], kb[blk]), 0.0) for blk in blocks]
        for level in range(nlevel):
            xs = scaled(level)
            b = 1 << level
            if b < 2 * sub:
                scores = [jnp.where(code == level, _dot_nt(xs[blk], xs[blk]), sc)
                          for blk, sc in zip(blocks, scores)]
                continue
            parity = 0 if reverse else 1
            query_rows = [slice(j * b, (j + 1) * b) for j in range(parity, SCAN_BLOCK // b, 2)]
            for i, blk in enumerate(blocks):
                xb = xs[blk]
                p = _dot_nt(jnp.concatenate([xb[r] for r in query_rows], 0), xb)
                pieces = []
                for j in range(SCAN_BLOCK // b):
                    rows = slice(j * b, (j + 1) * b)
                    if j % 2 == parity:
                        part = p[(j // 2) * b:(j // 2 + 1) * b]
                        pieces.append(jnp.where(code[rows] == level, part, scores[i][rows]))
                    else:
                        pieces.append(scores[i][rows])
                scores[i] = jnp.concatenate(pieces, 0)
        out = [_dot(sc.astype(BF16), vb[blk]) for blk, sc in zip(blocks, scores)]
        if nblk == 2:
            xs = scaled(nlevel)
            late, early = (0, 1) if reverse else (1, 0)
            cross = _dot_nt(xs[blocks[late]], xs[blocks[early]])
            out[late] = out[late] + _dot(cross.astype(BF16), vb[blocks[early]])
        o_intra = out[0] if nblk == 1 else jnp.concatenate(out, axis=0)

        st = st_ref[h]
        o_inter = _dot_nt((qf * jnp.exp2(c)).astype(BF16), st.astype(BF16))
        o_ref[:, ls] = o_intra + o_inter

        c_last = c_ref[pl.ds(last, 1), ls]
        ke = (kf * jnp.exp2(c_last - c)).astype(BF16)
        st_ref[h] = st * jnp.exp2(c_last) + _dot_tn(vb, ke)

    def head_group(g, carry):
        gl = pl.ds(pl.multiple_of(g * group * HEAD_DIM, group * HEAD_DIM), group * HEAD_DIM)
        lf = lf_ref[:, gl]
        hi = lf.astype(BF16)
        mid = (lf - hi.astype(F32)).astype(BF16)
        c_ref[:, gl] = _dot(tri, hi) + _dot(tri, mid)
        for i in range(group):
            head(g * group + i)
        return carry

    lax.fori_loop(0, heads // group, head_group, 0)


def _scan(reverse, batch, lf, q, k, v):
    t, width = lf.shape
    chunk = SCAN_CHUNK
    nc = t // batch // chunk
    tri, code = _scan_tables(chunk, reverse)
    if reverse:
        idx = lambda b, i: (b * nc + nc - 1 - i, 0)
    else:
        idx = lambda b, i: (b * nc + i, 0)
    row = pl.BlockSpec((chunk, width), idx)
    heads = width // HEAD_DIM
    return pl.pallas_call(
        functools.partial(_scan_kernel, reverse, SCAN_HEAD_GROUP),
        grid=(batch, nc),
        in_specs=[row, row, row, row, _resident(tri.shape), _resident(code.shape)],
        out_specs=row,
        out_shape=jax.ShapeDtypeStruct((t, width), F32),
        scratch_shapes=[pltpu.VMEM((heads, HEAD_DIM, HEAD_DIM), F32), pltpu.VMEM((chunk, width), F32)],
        compiler_params=pltpu.CompilerParams(dimension_semantics=("arbitrary", "arbitrary"),
                                             vmem_limit_bytes=VMEM_LIMIT),
        name="scan_bwd" if reverse else "scan_fwd",
    )(lf, q, k, v, tri, code)


def _mix_out_kernel(tiles_per_seq, x_ref, glu_ref, glu_prev_ref, glu_next_ref, of_ref, ob_ref, og_ref,
                    ga_ref, gb_ref, dww_ref, dwb_ref, lnw_ref, lnb_ref, pw_ref, hnw_ref, ow_ref, wout_ref,
                    xo_ref, ext_ref):
    tm, cw = glu_ref.shape
    i = pl.program_id(0)
    first = (i % tiles_per_seq) == 0
    final = (i % tiles_per_seq) == tiles_per_seq - 1

    ext_ref[pl.ds(0, CONV_HALO), :] = jnp.where(first, 0.0, glu_prev_ref[...])
    ext_ref[pl.ds(CONV_HALO, tm), :] = glu_ref[...]
    ext_ref[pl.ds(CONV_HALO + tm, CONV_HALO), :] = jnp.where(final, 0.0, glu_next_ref[...])
    base = CONV_HALO - CONV_K // 2
    acc = jnp.broadcast_to(dwb_ref[...], (tm, cw))
    for j in range(CONV_K):
        acc = acc + dww_ref[pl.ds(j, 1), :] * ext_ref[pl.ds(base + j, tm), :]
    mu = jnp.mean(acc, axis=-1, keepdims=True)
    cen = acc - mu
    var = jnp.mean(cen * cen, axis=-1, keepdims=True)
    a = _silu(cen * lax.rsqrt(var + EPS) * lnw_ref[...] + lnb_ref[...])
    a = _dot(a.astype(BF16), pw_ref[...])

    o = of_ref[...] + ob_ref[...]
    parts = []
    for h in range(o.shape[1] // HEAD_DIM):
        oh = o[:, h * HEAD_DIM:(h + 1) * HEAD_DIM]
        parts.append(oh * lax.rsqrt(jnp.mean(oh * oh, axis=-1, keepdims=True) + EPS))
    of = jnp.concatenate(parts, axis=1) * hnw_ref[...]
    bmix = _dot((of * og_ref[...].astype(F32)).astype(BF16), ow_ref[...])

    y = ga_ref[...].astype(F32) * a + gb_ref[...].astype(F32) * bmix
    xo_ref[...] = x_ref[...] + _dot(y.astype(BF16), wout_ref[...])


def _mix_out(seq, x2, glu, o_f, o_b, og, ga, gb, dw_w, dw_b, ln_w, ln_b, pw_b, hn_w, ow_b, wout_b):
    t, d = x2.shape
    cw = glu.shape[1]
    tm = TOKEN_TILE
    hb = tm // CONV_HALO
    nhalo = t // CONV_HALO
    row = lambda width: pl.BlockSpec((tm, width), lambda i: (i, 0))
    prev = pl.BlockSpec((CONV_HALO, cw), lambda i: (jnp.maximum(i * hb - 1, 0), 0))
    nxt = pl.BlockSpec((CONV_HALO, cw), lambda i: (jnp.minimum((i + 1) * hb, nhalo - 1), 0))
    weights = (dw_w, dw_b, ln_w, ln_b, pw_b, hn_w, ow_b, wout_b)
    return pl.pallas_call(
        functools.partial(_mix_out_kernel, seq // tm),
        grid=(t // tm,),
        in_specs=[row(d), row(cw), prev, nxt] + [row(d)] * 5 + [_resident(w.shape) for w in weights],
        out_specs=row(d),
        out_shape=jax.ShapeDtypeStruct((t, d), F32),
        scratch_shapes=[pltpu.VMEM((tm + 2 * CONV_HALO, cw), F32)],
        compiler_params=pltpu.CompilerParams(dimension_semantics=("arbitrary",),
                                             vmem_limit_bytes=VMEM_LIMIT),
        name="mix_out",
    )(x2, glu, glu, glu, o_f, o_b, og, ga, gb, *weights)


def _ffn_kernel(tiles_per_seq, ff_chunks, apply_final, x_ref, x_prev_ref, x_next_ref, nw_ref, wup_ref,
                dww_ref, dwb_ref, wdn_ref, fnw_ref, xo_ref, hext_ref, gext_ref):
    tm, d = x_ref.shape
    dff = wdn_ref.shape[0]
    fc = dff // ff_chunks
    i = pl.program_id(0)
    first = (i % tiles_per_seq) == 0
    final = (i % tiles_per_seq) == tiles_per_seq - 1

    def norm(x):
        return (x * lax.rsqrt(jnp.mean(x * x, axis=-1, keepdims=True) + EPS) * nw_ref[...]).astype(BF16)

    x = x_ref[...]
    hext_ref[pl.ds(0, FFN_HALO), :] = norm(x_prev_ref[...])
    hext_ref[pl.ds(FFN_HALO, tm), :] = norm(x)
    hext_ref[pl.ds(FFN_HALO + tm, FFN_HALO), :] = norm(x_next_ref[...])

    rows = lax.broadcasted_iota(jnp.int32, (tm + 2 * FFN_HALO, 1), 0)
    outside = (first & (rows < FFN_HALO)) | (final & (rows >= FFN_HALO + tm))
    base = FFN_HALO - FFN_CONV_K // 2
    acc = x
    for cidx in range(ff_chunks):
        lo = cidx * fc
        gate = _dot(hext_ref[...], wup_ref[:, lo:lo + fc])
        gext_ref[...] = jnp.where(outside, 0.0, gate)
        conv = jnp.broadcast_to(dwb_ref[:, lo:lo + fc], (tm, fc))
        for j in range(FFN_CONV_K):
            conv = conv + dww_ref[pl.ds(j, 1), lo:lo + fc] * gext_ref[pl.ds(base + j, tm), :]
        val = _dot(hext_ref[pl.ds(FFN_HALO, tm), :], wup_ref[:, dff + lo:dff + lo + fc])
        acc = acc + _dot((_silu(conv) * val).astype(BF16), wdn_ref[lo:lo + fc, :])
    if apply_final:
        acc = acc * lax.rsqrt(jnp.mean(acc * acc, axis=-1, keepdims=True) + EPS) * fnw_ref[...]
    xo_ref[...] = acc


def _ffn(seq, apply_final, x2, norm_w, wup_b, dw_w, dw_b, wdn_b, final_w):
    t, d = x2.shape
    dff = wdn_b.shape[0]
    tm = TOKEN_TILE
    ff_chunks = 2
    hb = tm // FFN_HALO
    nhalo = t // FFN_HALO
    row = pl.BlockSpec((tm, d), lambda i: (i, 0))
    prev = pl.BlockSpec((FFN_HALO, d), lambda i: (jnp.maximum(i * hb - 1, 0), 0))
    nxt = pl.BlockSpec((FFN_HALO, d), lambda i: (jnp.minimum((i + 1) * hb, nhalo - 1), 0))
    weights = (norm_w, wup_b, dw_w, dw_b, wdn_b, final_w)
    return pl.pallas_call(
        functools.partial(_ffn_kernel, seq // tm, ff_chunks, apply_final),
        grid=(t // tm,),
        in_specs=[row, prev, nxt] + [_resident(w.shape) for w in weights],
        out_specs=row,
        out_shape=jax.ShapeDtypeStruct((t, d), F32),
        scratch_shapes=[pltpu.VMEM((tm + 2 * FFN_HALO, d), BF16),
                        pltpu.VMEM((tm + 2 * FFN_HALO, dff // ff_chunks), F32)],
        compiler_params=pltpu.CompilerParams(dimension_semantics=("arbitrary",),
                                             vmem_limit_bytes=VMEM_LIMIT),
        name="ffn",
    )(x2, x2, x2, *weights)


def kernel(x, attn_norm_w, w_in, conv_dw_w, conv_dw_b, conv_ln_w, conv_ln_b, conv_pw_w, lb_logits,
           hgrn_norm_w, hgrn_o_w, w_out, ffn_norm_w, ffn_w_up, ffn_dw_w, ffn_dw_b, ffn_w_down, final_norm_w):
    batch, seq, d = x.shape
    depth = w_in.shape[0]
    x2 = x.reshape(batch * seq, d)
    lbl = lb_logits.astype(F32).reshape(depth * 2, -1)
    vec = lambda w: w.reshape(1, -1).astype(F32)
    for layer in range(depth):
        glu, q, v, lf_f, k_f, lf_b, k_b, og, ga, gb = _in_proj(
            layer, depth, x2, vec(attn_norm_w[layer]), w_in[layer].astype(BF16), lbl)
        o_f = _scan(False, batch, lf_f, q, k_f, v)
        o_b = _scan(True, batch, lf_b, q, k_b, v)
        x2 = _mix_out(seq, x2, glu, o_f, o_b, og, ga, gb,
                      conv_dw_w[layer], vec(conv_dw_b[layer]), vec(conv_ln_w[layer]), vec(conv_ln_b[layer]),
                      conv_pw_w[layer].astype(BF16), vec(hgrn_norm_w[layer]),
                      hgrn_o_w[layer].astype(BF16), w_out[layer].astype(BF16))
        x2 = _ffn(seq, layer == depth - 1, x2, vec(ffn_norm_w[layer]), ffn_w_up[layer].astype(BF16),
                  ffn_dw_w[layer], vec(ffn_dw_b[layer]), ffn_w_down[layer].astype(BF16), vec(final_norm_w))
    return x2.reshape(batch, seq, d)
```

```python
import functools

import numpy as np
import jax
import jax.numpy as jnp
from jax import lax
from jax.experimental import pallas as pl
from jax.experimental.pallas import tpu as pltpu

F32 = jnp.float32
BF16 = jnp.bfloat16

EPS = 1e-6
MIN_FORGET = 1e-6
HEAD_DIM = 128
CONV_K = 31
FFN_CONV_K = 3
CONV_HALO = 16
SUBLANES = 8
CONV_ROWS = 32
MXU_COLUMNS = 256
FFN_HALO = 8
SCAN_BLOCK = 128
SCAN_ROWS = 256
LOW_LEVELS = 3
TOKEN_TILE = 512
VMEM_LIMIT = 56 * 1024 * 1024


def _dot(a, b):
    return jnp.dot(a, b, preferred_element_type=F32)


def _dot_nt(a, b):
    return lax.dot_general(a, b, (((1,), (1,)), ((), ())), preferred_element_type=F32)


def _dot_tn(a, b):
    return lax.dot_general(a, b, (((0,), (0,)), ((), ())), preferred_element_type=F32)


def _sigmoid(x):
    return 0.5 * jnp.tanh(0.5 * x) + 0.5


def _silu(x):
    return x * _sigmoid(x)


def _resident(shape):
    nd = len(shape)
    return pl.BlockSpec(shape, lambda *_: (0,) * nd, pipeline_mode=pl.Buffered(1))


def _in_proj_kernel(layer, depth, x_ref, nw_ref, w_ref, lbl_ref,
                    glu_ref, q_ref, v_ref, lff_ref, kf_ref, lfb_ref, kb_ref, og_ref, ga_ref, gb_ref):
    d = x_ref.shape[1]
    x = x_ref[...]
    h = x * lax.rsqrt(jnp.mean(x * x, axis=-1, keepdims=True) + EPS) * nw_ref[...]
    hb = h.astype(BF16)

    def sec(lo, width):
        return _dot(hb, w_ref[:, lo:lo + width])

    half = d // 2
    glu_ref[...] = sec(0, half) * _sigmoid(sec(half, half))
    q = sec(d, d)
    q_ref[...] = (_silu(q) * (HEAD_DIM ** -0.5)).astype(q_ref.dtype)

    def lower_bound(direction):
        rows = [lbl_ref[2 * i + direction:2 * i + direction + 1, :] for i in range(depth)]
        m = functools.reduce(jnp.maximum, rows)
        e = [jnp.exp(r - m) for r in rows]
        den = functools.reduce(lambda a, b: a + b, e)
        p = [ei / den for ei in e]
        cum = functools.reduce(lambda a, b: a + b, p[:layer + 1])
        return cum - p[0]

    def forget(z, lb, lf_ref, k_ref):
        s = _sigmoid(z)
        f = lb + (1.0 - lb) * s
        lf_ref[...] = jnp.log2(jnp.clip(f, MIN_FORGET, 1.0))
        k_ref[...] = ((1.0 - lb) * (1.0 - s)).astype(k_ref.dtype)

    forget(sec(3 * d, d), lower_bound(0), lff_ref, kf_ref)
    forget(sec(4 * d, d), lower_bound(1), lfb_ref, kb_ref)
    og_ref[...] = _silu(sec(5 * d, d)).astype(og_ref.dtype)
    ga_ref[...] = _sigmoid(sec(6 * d, d)).astype(ga_ref.dtype)
    gb_ref[...] = _sigmoid(sec(7 * d, d)).astype(gb_ref.dtype)
    v_ref[...] = sec(2 * d, d).astype(v_ref.dtype)


def _in_proj(layer, depth, x2, norm_w, w_in_b, lbl):
    t, d = x2.shape
    tm = TOKEN_TILE
    row = lambda width: pl.BlockSpec((tm, width), lambda i: (i, 0))
    act = lambda dtype, width=d: jax.ShapeDtypeStruct((t, width), dtype)
    return pl.pallas_call(
        functools.partial(_in_proj_kernel, layer, depth),
        grid=(t // tm,),
        in_specs=[row(d), _resident((1, d)), _resident(w_in_b.shape), _resident(lbl.shape)],
        out_specs=[row(d // 2)] + [row(d)] * 9,
        out_shape=[act(F32, d // 2), act(BF16), act(BF16), act(F32), act(BF16), act(F32), act(BF16),
                   act(BF16), act(BF16), act(BF16)],
        compiler_params=pltpu.CompilerParams(dimension_semantics=("arbitrary",),
                                             vmem_limit_bytes=VMEM_LIMIT),
        name="in_proj",
    )(x2, norm_w, w_in_b, lbl)


def _scan_tables(reverse):
    b = np.arange(SCAN_BLOCK)
    tri = (b[None, :] >= b[:, None]) if reverse else (b[None, :] <= b[:, None])
    xor = b[:, None] ^ b[None, :]
    level = np.floor(np.log2(np.maximum(xor, 1))).astype(np.int32)
    before = (b[None, :] > b[:, None]) if reverse else (b[None, :] < b[:, None])
    code = np.where(before | (xor == 0), np.where(level < LOW_LEVELS, 0, level), -2).astype(np.int32)
    return jnp.asarray(tri, BF16), jnp.asarray(code)


def _scan_kernel(reverse, lf_ref, q_ref, k_ref, v_ref, tri_ref, code_ref, o_ref, st_ref, c_ref):
    rows_per_step, width = lf_ref.shape
    nblk = rows_per_step // SCAN_BLOCK
    nlevel = SCAN_BLOCK.bit_length() - 1
    heads = width // HEAD_DIM
    sub = SUBLANES

    @pl.when(pl.program_id(1) == 0)
    def _():
        st_ref[...] = jnp.zeros_like(st_ref)

    tri = tri_ref[...]
    code = code_ref[...]
    last = 0 if reverse else SCAN_BLOCK - 1
    parity = 0 if reverse else 1
    order = range(nblk - 1, -1, -1) if reverse else range(nblk)

    for i in range(nblk):
        blk = pl.ds(i * SCAN_BLOCK, SCAN_BLOCK)
        lf = lf_ref[blk, :]
        hi = lf.astype(BF16)
        mid = (lf - hi.astype(F32)).astype(BF16)
        c_ref[blk, :] = _dot(tri, hi) + _dot(tri, mid)

    def block(h, i):
        ls = pl.ds(h * HEAD_DIM, HEAD_DIM)
        base = i * SCAN_BLOCK
        blk = pl.ds(base, SCAN_BLOCK)
        c = c_ref[blk, ls]
        qf = q_ref[blk, ls].astype(F32)
        kf = k_ref[blk, ls].astype(F32)
        vb = v_ref[blk, ls]

        def ref_row(r, rows):
            return jnp.broadcast_to(c_ref[pl.ds(base + r, 1), ls], (rows, HEAD_DIM))

        mid_ref = jnp.concatenate([ref_row(g * sub + sub // 2, sub) for g in range(SCAN_BLOCK // sub)], 0)
        q_low = (qf * jnp.exp2(c - mid_ref)).astype(BF16)
        k_low = (kf * jnp.exp2(mid_ref - c)).astype(BF16)
        sc = jnp.where(code == 0, _dot_nt(q_low, k_low), 0.0)

        later, earlier = (kf, qf) if reverse else (qf, kf)
        for level in range(LOW_LEVELS, nlevel):
            b = 1 << level
            expo, mixed = [], []
            for j in range(SCAN_BLOCK // (2 * b)):
                ref = ref_row((2 * j + 1) * b, b)
                even = c[2 * j * b:(2 * j + 1) * b]
                odd = c[(2 * j + 1) * b:(2 * j + 2) * b]
                expo += [even - ref, ref - odd] if reverse else [ref - even, odd - ref]
                mixed += [earlier[2 * j * b:(2 * j + 1) * b], later[(2 * j + 1) * b:(2 * j + 2) * b]]
            xs = (jnp.concatenate(mixed, 0) * jnp.exp2(jnp.concatenate(expo, 0))).astype(BF16)
            if b < 2 * sub:
                sc = jnp.where(code == level, _dot_nt(xs, xs), sc)
                continue
            query = jnp.concatenate([xs[j * b:(j + 1) * b] for j in range(parity, SCAN_BLOCK // b, 2)], 0)
            p = _dot_nt(query, xs)
            pieces = []
            for j in range(SCAN_BLOCK // b):
                rows = slice(j * b, (j + 1) * b)
                if j % 2 == parity:
                    pieces.append(jnp.where(code[rows] == level, p[(j // 2) * b:(j // 2 + 1) * b], sc[rows]))
                else:
                    pieces.append(sc[rows])
            sc = jnp.concatenate(pieces, 0)

        st = st_ref[h]
        o_ref[blk, ls] = _dot(sc.astype(BF16), vb) + _dot_nt((qf * jnp.exp2(c)).astype(BF16), st.astype(BF16))
        c_last = c_ref[pl.ds(base + last, 1), ls]
        ke = (kf * jnp.exp2(c_last - c)).astype(BF16)
        st_ref[h] = st * jnp.exp2(c_last) + _dot_tn(vb, ke)

    for h in range(heads):
        for i in order:
            block(h, i)


def _scan(reverse, batch, lf, q, k, v):
    t, width = lf.shape
    rows = SCAN_ROWS
    nc = t // batch // rows
    tri, code = _scan_tables(reverse)
    if reverse:
        idx = lambda b, i: (b * nc + nc - 1 - i, 0)
    else:
        idx = lambda b, i: (b * nc + i, 0)
    row = pl.BlockSpec((rows, width), idx)
    heads = width // HEAD_DIM
    return pl.pallas_call(
        functools.partial(_scan_kernel, reverse),
        grid=(batch, nc),
        in_specs=[row, row, row, row, _resident(tri.shape), _resident(code.shape)],
        out_specs=row,
        out_shape=jax.ShapeDtypeStruct((t, width), F32),
        scratch_shapes=[pltpu.VMEM((heads, HEAD_DIM, HEAD_DIM), F32), pltpu.VMEM((rows, width), F32)],
        compiler_params=pltpu.CompilerParams(dimension_semantics=("arbitrary", "arbitrary"),
                                             vmem_limit_bytes=VMEM_LIMIT),
        name="scan_bwd" if reverse else "scan_fwd",
    )(lf, q, k, v, tri, code)


def _mix_out_kernel(tiles_per_seq, x_ref, glu_ref, glu_prev_ref, glu_next_ref, of_ref, ob_ref, og_ref,
                    ga_ref, gb_ref, dww_ref, dwb_ref, lnw_ref, lnb_ref, pw_ref, hnw_ref, ow_ref, wout_ref,
                    xo_ref, ext_ref, shift_ref, act_ref, taps_ref):
    tm, cw = glu_ref.shape
    i = pl.program_id(0)
    first = (i % tiles_per_seq) == 0
    final = (i % tiles_per_seq) == tiles_per_seq - 1

    ext_ref[pl.ds(0, CONV_HALO), :] = jnp.where(first, 0.0, glu_prev_ref[...])
    ext_ref[pl.ds(CONV_HALO, tm), :] = glu_ref[...]
    ext_ref[pl.ds(CONV_HALO + tm, CONV_HALO), :] = jnp.where(final, 0.0, glu_next_ref[...])
    ext = ext_ref[...]
    for p in range(1, SUBLANES):
        shift_ref[p - 1] = pltpu.roll(ext, ext.shape[0] - p, 0)
    base = CONV_HALO - CONV_K // 2
    for j in range(CONV_K):
        taps_ref[j] = jnp.broadcast_to(dww_ref[pl.ds(j, 1), :], (SUBLANES, cw))

    def conv_rows(r, carry):
        r0 = pl.multiple_of(r * CONV_ROWS, CONV_ROWS)
        acc = jnp.broadcast_to(dwb_ref[...], (CONV_ROWS, cw))
        for j in range(CONV_K):
            whole, p = divmod(base + j, SUBLANES)
            src = ext_ref if p == 0 else shift_ref.at[p - 1]
            tap = jnp.concatenate([taps_ref[j]] * (CONV_ROWS // SUBLANES), 0)
            acc = acc + tap * src[pl.ds(r0 + whole * SUBLANES, CONV_ROWS), :]
        mu = jnp.mean(acc, axis=-1, keepdims=True)
        cen = acc - mu
        var = jnp.mean(cen * cen, axis=-1, keepdims=True)
        act_ref[pl.ds(r0, CONV_ROWS), :] = _silu(
            cen * lax.rsqrt(var + EPS) * lnw_ref[...] + lnb_ref[...]).astype(BF16)
        return carry

    lax.fori_loop(0, tm // CONV_ROWS, conv_rows, 0, unroll=True)
    a = _dot(act_ref[...], pw_ref[...])

    o = of_ref[...] + ob_ref[...]
    parts = []
    for h in range(o.shape[1] // HEAD_DIM):
        oh = o[:, h * HEAD_DIM:(h + 1) * HEAD_DIM]
        parts.append(oh * lax.rsqrt(jnp.mean(oh * oh, axis=-1, keepdims=True) + EPS))
    of = jnp.concatenate(parts, axis=1) * hnw_ref[...]
    bmix = _dot((of * og_ref[...].astype(F32)).astype(BF16), ow_ref[...])

    y = ga_ref[...].astype(F32) * a + gb_ref[...].astype(F32) * bmix
    xo_ref[...] = x_ref[...] + _dot(y.astype(BF16), wout_ref[...])


def _mix_out(seq, x2, glu, o_f, o_b, og, ga, gb, dw_w, dw_b, ln_w, ln_b, pw_b, hn_w, ow_b, wout_b):
    t, d = x2.shape
    cw = glu.shape[1]
    tm = TOKEN_TILE
    hb = tm // CONV_HALO
    nhalo = t // CONV_HALO
    row = lambda width: pl.BlockSpec((tm, width), lambda i: (i, 0))
    prev = pl.BlockSpec((CONV_HALO, cw), lambda i: (jnp.maximum(i * hb - 1, 0), 0))
    nxt = pl.BlockSpec((CONV_HALO, cw), lambda i: (jnp.minimum((i + 1) * hb, nhalo - 1), 0))
    weights = (dw_w, dw_b, ln_w, ln_b, pw_b, hn_w, ow_b, wout_b)
    return pl.pallas_call(
        functools.partial(_mix_out_kernel, seq // tm),
        grid=(t // tm,),
        in_specs=[row(d), row(cw), prev, nxt] + [row(d)] * 5 + [_resident(w.shape) for w in weights],
        out_specs=row(d),
        out_shape=jax.ShapeDtypeStruct((t, d), F32),
        scratch_shapes=[pltpu.VMEM((tm + 2 * CONV_HALO, cw), F32),
                        pltpu.VMEM((SUBLANES - 1, tm + 2 * CONV_HALO, cw), F32),
                        pltpu.VMEM((tm, cw), BF16),
                        pltpu.VMEM((CONV_K, SUBLANES, cw), F32)],
        compiler_params=pltpu.CompilerParams(dimension_semantics=("arbitrary",),
                                             vmem_limit_bytes=VMEM_LIMIT),
        name="mix_out",
    )(x2, glu, glu, glu, o_f, o_b, og, ga, gb, *weights)


def _ffn_kernel(tiles_per_seq, ff_chunks, apply_final, x_ref, x_prev_ref, x_next_ref, nw_ref, wup_ref,
                dww_ref, dwb_ref, wdn_ref, fnw_ref, xo_ref, hext_ref, gext_ref):
    tm, d = x_ref.shape
    dff = wdn_ref.shape[0]
    i = pl.program_id(0)
    first = (i % tiles_per_seq) == 0
    final = (i % tiles_per_seq) == tiles_per_seq - 1

    def norm(x):
        return (x * lax.rsqrt(jnp.mean(x * x, axis=-1, keepdims=True) + EPS) * nw_ref[...]).astype(BF16)

    x = x_ref[...]
    hext_ref[pl.ds(0, FFN_HALO), :] = norm(x_prev_ref[...])
    hext_ref[pl.ds(FFN_HALO, tm), :] = norm(x)
    hext_ref[pl.ds(FFN_HALO + tm, FFN_HALO), :] = norm(x_next_ref[...])

    rows = lax.broadcasted_iota(jnp.int32, (tm + 2 * FFN_HALO, 1), 0)
    outside = (first & (rows < FFN_HALO)) | (final & (rows >= FFN_HALO + tm))
    base = FFN_HALO - FFN_CONV_K // 2
    acc = x
    lo = 0
    for fc in ff_chunks:
        gate = _dot(hext_ref[...], wup_ref[:, lo:lo + fc])
        gext_ref[:, 0:fc] = jnp.where(outside, 0.0, gate)
        conv = jnp.broadcast_to(dwb_ref[:, lo:lo + fc], (tm, fc))
        for j in range(FFN_CONV_K):
            conv = conv + dww_ref[pl.ds(j, 1), lo:lo + fc] * gext_ref[pl.ds(base + j, tm), 0:fc]
        val = _dot(hext_ref[pl.ds(FFN_HALO, tm), :], wup_ref[:, dff + lo:dff + lo + fc])
        acc = acc + _dot((_silu(conv) * val).astype(BF16), wdn_ref[lo:lo + fc, :])
        lo += fc
    if apply_final:
        acc = acc * lax.rsqrt(jnp.mean(acc * acc, axis=-1, keepdims=True) + EPS) * fnw_ref[...]
    xo_ref[...] = acc


def _ffn_chunks(dff):
    tiles = dff // MXU_COLUMNS
    assert tiles * MXU_COLUMNS == dff
    first = (tiles + 1) // 2 * MXU_COLUMNS
    return (first, dff - first)


def _ffn(seq, apply_final, x2, norm_w, wup_b, dw_w, dw_b, wdn_b, final_w):
    t, d = x2.shape
    dff = wdn_b.shape[0]
    tm = TOKEN_TILE
    ff_chunks = _ffn_chunks(dff)
    hb = tm // FFN_HALO
    nhalo = t // FFN_HALO
    row = pl.BlockSpec((tm, d), lambda i: (i, 0))
    prev = pl.BlockSpec((FFN_HALO, d), lambda i: (jnp.maximum(i * hb - 1, 0), 0))
    nxt = pl.BlockSpec((FFN_HALO, d), lambda i: (jnp.minimum((i + 1) * hb, nhalo - 1), 0))
    weights = (norm_w, wup_b, dw_w, dw_b, wdn_b, final_w)
    return pl.pallas_call(
        functools.partial(_ffn_kernel, seq // tm, ff_chunks, apply_final),
        grid=(t // tm,),
        in_specs=[row, prev, nxt] + [_resident(w.shape) for w in weights],
        out_specs=row,
        out_shape=jax.ShapeDtypeStruct((t, d), F32),
        scratch_shapes=[pltpu.VMEM((tm + 2 * FFN_HALO, d), BF16),
                        pltpu.VMEM((tm + 2 * FFN_HALO, max(ff_chunks)), F32)],
        compiler_params=pltpu.CompilerParams(dimension_semantics=("arbitrary",),
                                             vmem_limit_bytes=VMEM_LIMIT),
        name="ffn",
    )(x2, x2, x2, *weights)


def kernel(x, attn_norm_w, w_in, conv_dw_w, conv_dw_b, conv_ln_w, conv_ln_b, conv_pw_w, lb_logits,
           hgrn_norm_w, hgrn_o_w, w_out, ffn_norm_w, ffn_w_up, ffn_dw_w, ffn_dw_b, ffn_w_down, final_norm_w):
    batch, seq, d = x.shape
    depth = w_in.shape[0]
    x2 = x.reshape(batch * seq, d)
    lbl = lb_logits.astype(F32).reshape(depth * 2, -1)
    vec = lambda w: w.reshape(1, -1).astype(F32)
    for layer in range(depth):
        glu, q, v, lf_f, k_f, lf_b, k_b, og, ga, gb = _in_proj(
            layer, depth, x2, vec(attn_norm_w[layer]), w_in[layer].astype(BF16), lbl)
        o_f = _scan(False, batch, lf_f, q, k_f, v)
        o_b = _scan(True, batch, lf_b, q, k_b, v)
        x2 = _mix_out(seq, x2, glu, o_f, o_b, og, ga, gb,
                      conv_dw_w[layer], vec(conv_dw_b[layer]), vec(conv_ln_w[layer]), vec(conv_ln_b[layer]),
                      conv_pw_w[layer].astype(BF16), vec(hgrn_norm_w[layer]),
                      hgrn_o_w[layer].astype(BF16), w_out[layer].astype(BF16))
        x2 = _ffn(seq, layer == depth - 1, x2, vec(ffn_norm_w[layer]), ffn_w_up[layer].astype(BF16),
                  ffn_dw_w[layer], vec(ffn_dw_b[layer]), ffn_w_down[layer].astype(BF16), vec(final_norm_w))
    return x2.reshape(batch, seq, d)
```

```python
import functools

import numpy as np
import jax
import jax.numpy as jnp
from jax import lax
from jax.experimental import pallas as pl
from jax.experimental.pallas import tpu as pltpu

F32 = jnp.float32
BF16 = jnp.bfloat16

EPS = 1e-6
MIN_FORGET = 1e-6
HEAD_DIM = 128
CONV_K = 31
FFN_CONV_K = 3
CONV_HALO = 16
SUBLANES = 8
CONV_ROWS = 32
MXU_COLUMNS = 256
FFN_HALO = 8
SCAN_BLOCK = 128
SCAN_ROWS = 256
SCAN_SKEW = 1
LOW_LEVELS = 3
TOKEN_TILE = 512
VMEM_LIMIT = 56 * 1024 * 1024


def _dot(a, b):
    return jnp.dot(a, b, preferred_element_type=F32)


def _dot_nt(a, b):
    return lax.dot_general(a, b, (((1,), (1,)), ((), ())), preferred_element_type=F32)


def _dot_tn(a, b):
    return lax.dot_general(a, b, (((0,), (0,)), ((), ())), preferred_element_type=F32)


def _sigmoid(x):
    return 0.5 * jnp.tanh(0.5 * x) + 0.5


def _silu(x):
    return x * _sigmoid(x)


def _resident(shape):
    nd = len(shape)
    return pl.BlockSpec(shape, lambda *_: (0,) * nd, pipeline_mode=pl.Buffered(1))


def _in_proj_kernel(layer, depth, x_ref, nw_ref, w_ref, lbl_ref,
                    glu_ref, q_ref, v_ref, lff_ref, kf_ref, lfb_ref, kb_ref, og_ref, ga_ref, gb_ref):
    d = x_ref.shape[1]
    x = x_ref[...]
    h = x * lax.rsqrt(jnp.mean(x * x, axis=-1, keepdims=True) + EPS) * nw_ref[...]
    hb = h.astype(BF16)

    def sec(lo, width):
        return _dot(hb, w_ref[:, lo:lo + width])

    half = d // 2
    glu_ref[...] = sec(0, half) * _sigmoid(sec(half, half))
    q = sec(d, d)
    q_ref[...] = (_silu(q) * (HEAD_DIM ** -0.5)).astype(q_ref.dtype)

    def lower_bound(direction):
        rows = [lbl_ref[2 * i + direction:2 * i + direction + 1, :] for i in range(depth)]
        m = functools.reduce(jnp.maximum, rows)
        e = [jnp.exp(r - m) for r in rows]
        den = functools.reduce(lambda a, b: a + b, e)
        p = [ei / den for ei in e]
        cum = functools.reduce(lambda a, b: a + b, p[:layer + 1])
        return cum - p[0]

    def forget(z, lb, lf_ref, k_ref):
        s = _sigmoid(z)
        f = lb + (1.0 - lb) * s
        lf_ref[...] = jnp.log2(jnp.clip(f, MIN_FORGET, 1.0))
        k_ref[...] = ((1.0 - lb) * (1.0 - s)).astype(k_ref.dtype)

    forget(sec(3 * d, d), lower_bound(0), lff_ref, kf_ref)
    forget(sec(4 * d, d), lower_bound(1), lfb_ref, kb_ref)
    og_ref[...] = _silu(sec(5 * d, d)).astype(og_ref.dtype)
    ga_ref[...] = _sigmoid(sec(6 * d, d)).astype(ga_ref.dtype)
    gb_ref[...] = _sigmoid(sec(7 * d, d)).astype(gb_ref.dtype)
    v_ref[...] = sec(2 * d, d).astype(v_ref.dtype)


def _in_proj(layer, depth, x2, norm_w, w_in_b, lbl):
    t, d = x2.shape
    tm = TOKEN_TILE
    row = lambda width: pl.BlockSpec((tm, width), lambda i: (i, 0))
    act = lambda dtype, width=d: jax.ShapeDtypeStruct((t, width), dtype)
    return pl.pallas_call(
        functools.partial(_in_proj_kernel, layer, depth),
        grid=(t // tm,),
        in_specs=[row(d), _resident((1, d)), _resident(w_in_b.shape), _resident(lbl.shape)],
        out_specs=[row(d // 2)] + [row(d)] * 9,
        out_shape=[act(F32, d // 2), act(BF16), act(BF16), act(F32), act(BF16), act(F32), act(BF16),
                   act(BF16), act(BF16), act(BF16)],
        compiler_params=pltpu.CompilerParams(dimension_semantics=("arbitrary",),
                                             vmem_limit_bytes=VMEM_LIMIT),
        name="in_proj",
    )(x2, norm_w, w_in_b, lbl)


def _scan_tables(reverse):
    b = np.arange(SCAN_BLOCK)
    tri = (b[None, :] >= b[:, None]) if reverse else (b[None, :] <= b[:, None])
    xor = b[:, None] ^ b[None, :]
    level = np.floor(np.log2(np.maximum(xor, 1))).astype(np.int32)
    before = (b[None, :] > b[:, None]) if reverse else (b[None, :] < b[:, None])
    code = np.where(before | (xor == 0), np.where(level < LOW_LEVELS, 0, level), -2).astype(np.int32)
    return jnp.asarray(tri, BF16), jnp.asarray(code)


def _scan_kernel(reverse, lf_ref, q_ref, k_ref, v_ref, tri_ref, code_ref, o_ref, st_ref, c_ref, xt_ref):
    rows_per_step, width = lf_ref.shape
    nblk = rows_per_step // SCAN_BLOCK
    nlevel = SCAN_BLOCK.bit_length() - 1
    heads = width // HEAD_DIM
    sub = SUBLANES

    @pl.when(pl.program_id(1) == 0)
    def _():
        st_ref[...] = jnp.zeros_like(st_ref)

    tri = tri_ref[...]
    code = code_ref[...]
    last = 0 if reverse else SCAN_BLOCK - 1
    parity = 0 if reverse else 1
    order = range(nblk - 1, -1, -1) if reverse else range(nblk)

    for i in range(nblk):
        blk = pl.ds(i * SCAN_BLOCK, SCAN_BLOCK)
        lf = lf_ref[blk, :]
        hi = lf.astype(BF16)
        mid = (lf - hi.astype(F32)).astype(BF16)
        c_ref[blk, :] = _dot(tri, hi) + _dot(tri, mid)

    def block(h, i):
        ls = pl.ds(h * HEAD_DIM, HEAD_DIM)
        base = i * SCAN_BLOCK
        blk = pl.ds(base, SCAN_BLOCK)
        c = c_ref[blk, ls]
        qf = q_ref[blk, ls].astype(F32)
        kf = k_ref[blk, ls].astype(F32)
        vb = v_ref[blk, ls]

        def ref_row(r, rows):
            return jnp.broadcast_to(c_ref[pl.ds(base + r, 1), ls], (rows, HEAD_DIM))

        mid_ref = jnp.concatenate([ref_row(g * sub + sub // 2, sub) for g in range(SCAN_BLOCK // sub)], 0)
        q_low = (qf * jnp.exp2(c - mid_ref)).astype(BF16)
        k_low = (kf * jnp.exp2(mid_ref - c)).astype(BF16)
        slot = (h * nblk + i) * nlevel
        xt_ref[slot] = k_low.T
        sc = jnp.where(code == 0, _dot(q_low, xt_ref[slot]), 0.0)

        later, earlier = (kf, qf) if reverse else (qf, kf)
        for level in range(LOW_LEVELS, nlevel):
            b = 1 << level
            expo, mixed = [], []
            for j in range(SCAN_BLOCK // (2 * b)):
                ref = ref_row((2 * j + 1) * b, b)
                even = c[2 * j * b:(2 * j + 1) * b]
                odd = c[(2 * j + 1) * b:(2 * j + 2) * b]
                expo += [even - ref, ref - odd] if reverse else [ref - even, odd - ref]
                mixed += [earlier[2 * j * b:(2 * j + 1) * b], later[(2 * j + 1) * b:(2 * j + 2) * b]]
            xs = (jnp.concatenate(mixed, 0) * jnp.exp2(jnp.concatenate(expo, 0))).astype(BF16)
            xt_ref[slot + level] = xs.T
            xt = xt_ref[slot + level]
            if b < 2 * sub:
                sc = jnp.where(code == level, _dot(xs, xt), sc)
                continue
            query = jnp.concatenate([xs[j * b:(j + 1) * b] for j in range(parity, SCAN_BLOCK // b, 2)], 0)
            p = _dot(query, xt)
            pieces = []
            for j in range(SCAN_BLOCK // b):
                rows = slice(j * b, (j + 1) * b)
                if j % 2 == parity:
                    pieces.append(jnp.where(code[rows] == level, p[(j // 2) * b:(j // 2 + 1) * b], sc[rows]))
                else:
                    pieces.append(sc[rows])
            sc = jnp.concatenate(pieces, 0)

        c_last = c_ref[pl.ds(base + last, 1), ls]
        qe = (qf * jnp.exp2(c)).astype(BF16)
        ke = (kf * jnp.exp2(c_last - c)).astype(BF16)
        return sc.astype(BF16), qe, ke, vb, jnp.exp2(c_last)

    def finish(h, i, sc, qe, ke, vb, decay):
        ls = pl.ds(h * HEAD_DIM, HEAD_DIM)
        blk = pl.ds(i * SCAN_BLOCK, SCAN_BLOCK)
        st = st_ref[h]
        o_ref[blk, ls] = _dot(sc, vb) + _dot_nt(qe, st.astype(BF16))
        st_ref[h] = st * decay + _dot_tn(vb, ke)

    units = [(h, i) for i in order for h in range(heads)]
    pending = []
    for h, i in units:
        pending.append((h, i) + block(h, i))
        if len(pending) > SCAN_SKEW:
            finish(*pending.pop(0))
    for unit in pending:
        finish(*unit)


def _scan(reverse, batch, lf, q, k, v):
    t, width = lf.shape
    rows = SCAN_ROWS
    nc = t // batch // rows
    tri, code = _scan_tables(reverse)
    if reverse:
        idx = lambda b, i: (b * nc + nc - 1 - i, 0)
    else:
        idx = lambda b, i: (b * nc + i, 0)
    row = pl.BlockSpec((rows, width), idx)
    heads = width // HEAD_DIM
    return pl.pallas_call(
        functools.partial(_scan_kernel, reverse),
        grid=(batch, nc),
        in_specs=[row, row, row, row, _resident(tri.shape), _resident(code.shape)],
        out_specs=row,
        out_shape=jax.ShapeDtypeStruct((t, width), F32),
        scratch_shapes=[pltpu.VMEM((heads, HEAD_DIM, HEAD_DIM), F32), pltpu.VMEM((rows, width), F32),
                        pltpu.VMEM((heads * (rows // SCAN_BLOCK) * (SCAN_BLOCK.bit_length() - 1), HEAD_DIM, SCAN_BLOCK),
                                   BF16)],
        compiler_params=pltpu.CompilerParams(dimension_semantics=("arbitrary", "arbitrary"),
                                             vmem_limit_bytes=VMEM_LIMIT),
        name="scan_bwd" if reverse else "scan_fwd",
    )(lf, q, k, v, tri, code)


def _mix_out_kernel(tiles_per_seq, x_ref, glu_ref, glu_prev_ref, glu_next_ref, of_ref, ob_ref, og_ref,
                    ga_ref, gb_ref, dww_ref, dwb_ref, lnw_ref, lnb_ref, pw_ref, hnw_ref, ow_ref, wout_ref,
                    xo_ref, ext_ref, shift_ref, act_ref, taps_ref):
    tm, cw = glu_ref.shape
    i = pl.program_id(0)
    first = (i % tiles_per_seq) == 0
    final = (i % tiles_per_seq) == tiles_per_seq - 1

    ext_ref[pl.ds(0, CONV_HALO), :] = jnp.where(first, 0.0, glu_prev_ref[...])
    ext_ref[pl.ds(CONV_HALO, tm), :] = glu_ref[...]
    ext_ref[pl.ds(CONV_HALO + tm, CONV_HALO), :] = jnp.where(final, 0.0, glu_next_ref[...])
    ext = ext_ref[...]
    for p in range(1, SUBLANES):
        shift_ref[p - 1] = pltpu.roll(ext, ext.shape[0] - p, 0)
    base = CONV_HALO - CONV_K // 2
    for j in range(CONV_K):
        taps_ref[j] = jnp.broadcast_to(dww_ref[pl.ds(j, 1), :], (SUBLANES, cw))

    def conv_rows(r):
        r0 = r * CONV_ROWS
        acc = jnp.broadcast_to(dwb_ref[...], (CONV_ROWS, cw))
        for j in range(CONV_K):
            whole, p = divmod(base + j, SUBLANES)
            src = ext_ref if p == 0 else shift_ref.at[p - 1]
            tap = jnp.concatenate([taps_ref[j]] * (CONV_ROWS // SUBLANES), 0)
            acc = acc + tap * src[pl.ds(r0 + whole * SUBLANES, CONV_ROWS), :]
        mu = jnp.mean(acc, axis=-1, keepdims=True)
        cen = acc - mu
        var = jnp.mean(cen * cen, axis=-1, keepdims=True)
        act_ref[pl.ds(r0, CONV_ROWS), :] = _silu(
            cen * lax.rsqrt(var + EPS) * lnw_ref[...] + lnb_ref[...]).astype(BF16)

    o = of_ref[...] + ob_ref[...]
    parts = []
    for h in range(o.shape[1] // HEAD_DIM):
        oh = o[:, h * HEAD_DIM:(h + 1) * HEAD_DIM]
        parts.append(oh * lax.rsqrt(jnp.mean(oh * oh, axis=-1, keepdims=True) + EPS))
    of = jnp.concatenate(parts, axis=1) * hnw_ref[...]
    gated_b = gb_ref[...].astype(F32) * _dot((of * og_ref[...].astype(F32)).astype(BF16), ow_ref[...])

    halves = 2
    rows_per_half = tm // halves
    for half in range(halves):
        for r in range(half * rows_per_half // CONV_ROWS, (half + 1) * rows_per_half // CONV_ROWS):
            conv_rows(r)
        rows = pl.ds(half * rows_per_half, rows_per_half)
        a = _dot(act_ref[rows, :], pw_ref[...])
        y = ga_ref[rows, :].astype(F32) * a + gated_b[half * rows_per_half:(half + 1) * rows_per_half]
        xo_ref[rows, :] = x_ref[rows, :] + _dot(y.astype(BF16), wout_ref[...])


def _mix_out(seq, x2, glu, o_f, o_b, og, ga, gb, dw_w, dw_b, ln_w, ln_b, pw_b, hn_w, ow_b, wout_b):
    t, d = x2.shape
    cw = glu.shape[1]
    tm = TOKEN_TILE
    hb = tm // CONV_HALO
    nhalo = t // CONV_HALO
    row = lambda width: pl.BlockSpec((tm, width), lambda i: (i, 0))
    prev = pl.BlockSpec((CONV_HALO, cw), lambda i: (jnp.maximum(i * hb - 1, 0), 0))
    nxt = pl.BlockSpec((CONV_HALO, cw), lambda i: (jnp.minimum((i + 1) * hb, nhalo - 1), 0))
    weights = (dw_w, dw_b, ln_w, ln_b, pw_b, hn_w, ow_b, wout_b)
    return pl.pallas_call(
        functools.partial(_mix_out_kernel, seq // tm),
        grid=(t // tm,),
        in_specs=[row(d), row(cw), prev, nxt] + [row(d)] * 5 + [_resident(w.shape) for w in weights],
        out_specs=row(d),
        out_shape=jax.ShapeDtypeStruct((t, d), F32),
        scratch_shapes=[pltpu.VMEM((tm + 2 * CONV_HALO, cw), F32),
                        pltpu.VMEM((SUBLANES - 1, tm + 2 * CONV_HALO, cw), F32),
                        pltpu.VMEM((tm, cw), BF16),
                        pltpu.VMEM((CONV_K, SUBLANES, cw), F32)],
        compiler_params=pltpu.CompilerParams(dimension_semantics=("arbitrary",),
                                             vmem_limit_bytes=VMEM_LIMIT),
        name="mix_out",
    )(x2, glu, glu, glu, o_f, o_b, og, ga, gb, *weights)


def _ffn_kernel(tiles_per_seq, ff_chunks, apply_final, x_ref, x_prev_ref, x_next_ref, nw_ref, wup_ref,
                dww_ref, dwb_ref, wdn_ref, fnw_ref, xo_ref, hext_ref, gext_ref):
    tm, d = x_ref.shape
    dff = wdn_ref.shape[0]
    i = pl.program_id(0)
    first = (i % tiles_per_seq) == 0
    final = (i % tiles_per_seq) == tiles_per_seq - 1

    def norm(x):
        return (x * lax.rsqrt(jnp.mean(x * x, axis=-1, keepdims=True) + EPS) * nw_ref[...]).astype(BF16)

    x = x_ref[...]
    hext_ref[pl.ds(0, FFN_HALO), :] = norm(x_prev_ref[...])
    hext_ref[pl.ds(FFN_HALO, tm), :] = norm(x)
    hext_ref[pl.ds(FFN_HALO + tm, FFN_HALO), :] = norm(x_next_ref[...])

    rows = lax.broadcasted_iota(jnp.int32, (tm + 2 * FFN_HALO, 1), 0)
    outside = (first & (rows < FFN_HALO)) | (final & (rows >= FFN_HALO + tm))
    base = FFN_HALO - FFN_CONV_K // 2
    hidden = []
    lo = 0
    for fc in ff_chunks:
        gate = _dot(hext_ref[...], wup_ref[:, lo:lo + fc])
        gext_ref[:, lo:lo + fc] = jnp.where(outside, 0.0, gate)
        conv = jnp.broadcast_to(dwb_ref[:, lo:lo + fc], (tm, fc))
        for j in range(FFN_CONV_K):
            conv = conv + dww_ref[pl.ds(j, 1), lo:lo + fc] * gext_ref[pl.ds(base + j, tm), lo:lo + fc]
        val = _dot(hext_ref[pl.ds(FFN_HALO, tm), :], wup_ref[:, dff + lo:dff + lo + fc])
        hidden.append((_silu(conv) * val).astype(BF16))
        lo += fc
    acc = x
    lo = 0
    for fc, u in zip(ff_chunks, hidden):
        acc = acc + _dot(u, wdn_ref[lo:lo + fc, :])
        lo += fc
    if apply_final:
        acc = acc * lax.rsqrt(jnp.mean(acc * acc, axis=-1, keepdims=True) + EPS) * fnw_ref[...]
    xo_ref[...] = acc


def _ffn_chunks(dff):
    tiles = dff // MXU_COLUMNS
    assert tiles * MXU_COLUMNS == dff
    first = (tiles + 1) // 2 * MXU_COLUMNS
    return (first, dff - first)


def _ffn(seq, apply_final, x2, norm_w, wup_b, dw_w, dw_b, wdn_b, final_w):
    t, d = x2.shape
    dff = wdn_b.shape[0]
    tm = TOKEN_TILE
    ff_chunks = _ffn_chunks(dff)
    hb = tm // FFN_HALO
    nhalo = t // FFN_HALO
    row = pl.BlockSpec((tm, d), lambda i: (i, 0))
    prev = pl.BlockSpec((FFN_HALO, d), lambda i: (jnp.maximum(i * hb - 1, 0), 0))
    nxt = pl.BlockSpec((FFN_HALO, d), lambda i: (jnp.minimum((i + 1) * hb, nhalo - 1), 0))
    weights = (norm_w, wup_b, dw_w, dw_b, wdn_b, final_w)
    return pl.pallas_call(
        functools.partial(_ffn_kernel, seq // tm, ff_chunks, apply_final),
        grid=(t // tm,),
        in_specs=[row, prev, nxt] + [_resident(w.shape) for w in weights],
        out_specs=row,
        out_shape=jax.ShapeDtypeStruct((t, d), F32),
        scratch_shapes=[pltpu.VMEM((tm + 2 * FFN_HALO, d), BF16),
                        pltpu.VMEM((tm + 2 * FFN_HALO, dff), F32)],
        compiler_params=pltpu.CompilerParams(dimension_semantics=("arbitrary",),
                                             vmem_limit_bytes=VMEM_LIMIT),
        name="ffn",
    )(x2, x2, x2, *weights)


def kernel(x, attn_norm_w, w_in, conv_dw_w, conv_dw_b, conv_ln_w, conv_ln_b, conv_pw_w, lb_logits,
           hgrn_norm_w, hgrn_o_w, w_out, ffn_norm_w, ffn_w_up, ffn_dw_w, ffn_dw_b, ffn_w_down, final_norm_w):
    batch, seq, d = x.shape
    depth = w_in.shape[0]
    x2 = x.reshape(batch * seq, d)
    lbl = lb_logits.astype(F32).reshape(depth * 2, -1)
    vec = lambda w: w.reshape(1, -1).astype(F32)
    for layer in range(depth):
        glu, q, v, lf_f, k_f, lf_b, k_b, og, ga, gb = _in_proj(
            layer, depth, x2, vec(attn_norm_w[layer]), w_in[layer].astype(BF16), lbl)
        o_f = _scan(False, batch, lf_f, q, k_f, v)
        o_b = _scan(True, batch, lf_b, q, k_b, v)
        x2 = _mix_out(seq, x2, glu, o_f, o_b, og, ga, gb,
                      conv_dw_w[layer], vec(conv_dw_b[layer]), vec(conv_ln_w[layer]), vec(conv_ln_b[layer]),
                      conv_pw_w[layer].astype(BF16), vec(hgrn_norm_w[layer]),
                      hgrn_o_w[layer].astype(BF16), w_out[layer].astype(BF16))
        x2 = _ffn(seq, layer == depth - 1, x2, vec(ffn_norm_w[layer]), ffn_w_up[layer].astype(BF16),
                  ffn_dw_w[layer], vec(ffn_dw_b[layer]), ffn_w_down[layer].astype(BF16), vec(final_norm_w))
    return x2.reshape(batch, seq, d)
```

```python
import functools

import numpy as np
import jax
import jax.numpy as jnp
from jax import lax
from jax.experimental import pallas as pl
from jax.experimental.pallas import tpu as pltpu

F32 = jnp.float32
BF16 = jnp.bfloat16

EPS = 1e-6
MIN_FORGET = 1e-6
HEAD_DIM = 128
CONV_K = 31
FFN_CONV_K = 3
CONV_HALO = 16
SUBLANES = 8
CONV_ROWS = 32
MXU_COLUMNS = 256
FFN_HALO = 8
SCAN_BLOCK = 128
SCAN_ROWS = 512
SCAN_SKEW = 1
LOW_LEVELS = 3
TOKEN_TILE = 512
VMEM_LIMIT = 56 * 1024 * 1024


def _dot(a, b):
    return jnp.dot(a, b, preferred_element_type=F32)


def _dot_nt(a, b):
    return lax.dot_general(a, b, (((1,), (1,)), ((), ())), preferred_element_type=F32)


def _dot_tn(a, b):
    return lax.dot_general(a, b, (((0,), (0,)), ((), ())), preferred_element_type=F32)


def _sigmoid(x):
    return 0.5 * jnp.tanh(0.5 * x) + 0.5


def _silu(x):
    return x * _sigmoid(x)


def _resident(shape):
    nd = len(shape)
    return pl.BlockSpec(shape, lambda *_: (0,) * nd, pipeline_mode=pl.Buffered(1))


def _in_proj_kernel(layer, depth, x_ref, nw_ref, w_ref, lbl_ref,
                    glu_ref, q_ref, v_ref, lff_ref, kf_ref, lfb_ref, kb_ref, og_ref, ga_ref, gb_ref):
    d = x_ref.shape[1]
    x = x_ref[...]
    h = x * lax.rsqrt(jnp.mean(x * x, axis=-1, keepdims=True) + EPS) * nw_ref[...]
    hb = h.astype(BF16)

    def sec(lo, width):
        return _dot(hb, w_ref[:, lo:lo + width])

    half = d // 2
    glu_ref[...] = sec(0, half) * _sigmoid(sec(half, half))
    q = sec(d, d)
    q_ref[...] = (_silu(q) * (HEAD_DIM ** -0.5)).astype(q_ref.dtype)

    def lower_bound(direction):
        rows = [lbl_ref[2 * i + direction:2 * i + direction + 1, :] for i in range(depth)]
        m = functools.reduce(jnp.maximum, rows)
        e = [jnp.exp(r - m) for r in rows]
        den = functools.reduce(lambda a, b: a + b, e)
        p = [ei / den for ei in e]
        cum = functools.reduce(lambda a, b: a + b, p[:layer + 1])
        return cum - p[0]

    def forget(z, lb, lf_ref, k_ref):
        s = _sigmoid(z)
        f = lb + (1.0 - lb) * s
        lf_ref[...] = jnp.log2(jnp.clip(f, MIN_FORGET, 1.0))
        k_ref[...] = ((1.0 - lb) * (1.0 - s)).astype(k_ref.dtype)

    forget(sec(3 * d, d), lower_bound(0), lff_ref, kf_ref)
    forget(sec(4 * d, d), lower_bound(1), lfb_ref, kb_ref)
    og_ref[...] = _silu(sec(5 * d, d)).astype(og_ref.dtype)
    ga_ref[...] = _sigmoid(sec(6 * d, d)).astype(ga_ref.dtype)
    gb_ref[...] = _sigmoid(sec(7 * d, d)).astype(gb_ref.dtype)
    v_ref[...] = sec(2 * d, d).astype(v_ref.dtype)


def _in_proj(layer, depth, x2, norm_w, w_in_b, lbl):
    t, d = x2.shape
    tm = TOKEN_TILE
    row = lambda width: pl.BlockSpec((tm, width), lambda i: (i, 0))
    act = lambda dtype, width=d: jax.ShapeDtypeStruct((t, width), dtype)
    return pl.pallas_call(
        functools.partial(_in_proj_kernel, layer, depth),
        grid=(t // tm,),
        in_specs=[row(d), _resident((1, d)), _resident(w_in_b.shape), _resident(lbl.shape)],
        out_specs=[row(d // 2)] + [row(d)] * 9,
        out_shape=[act(F32, d // 2), act(BF16), act(BF16), act(F32), act(BF16), act(F32), act(BF16),
                   act(BF16), act(BF16), act(BF16)],
        compiler_params=pltpu.CompilerParams(dimension_semantics=("arbitrary",),
                                             vmem_limit_bytes=VMEM_LIMIT),
        name="in_proj",
    )(x2, norm_w, w_in_b, lbl)


def _scan_tables(reverse):
    b = np.arange(SCAN_BLOCK)
    tri = (b[None, :] >= b[:, None]) if reverse else (b[None, :] <= b[:, None])
    xor = b[:, None] ^ b[None, :]
    level = np.floor(np.log2(np.maximum(xor, 1))).astype(np.int32)
    before = (b[None, :] > b[:, None]) if reverse else (b[None, :] < b[:, None])
    code = np.where(before | (xor == 0), np.where(level < LOW_LEVELS, 0, level), -2).astype(np.int32)
    return jnp.asarray(tri, BF16), jnp.asarray(code)


def _scan_kernel(reverse, accumulate, lf_ref, q_ref, k_ref, v_ref, tri_ref, code_ref, *rest):
    acc_ref, (o_ref, st_ref, c_ref, xt_ref) = (rest[0], rest[1:]) if accumulate else (None, rest)
    rows_per_step, width = lf_ref.shape
    nblk = rows_per_step // SCAN_BLOCK
    nlevel = SCAN_BLOCK.bit_length() - 1
    heads = width // HEAD_DIM
    sub = SUBLANES

    @pl.when(pl.program_id(1) == 0)
    def _():
        st_ref[...] = jnp.zeros_like(st_ref)

    tri = tri_ref[...]
    code = code_ref[...]
    last = 0 if reverse else SCAN_BLOCK - 1
    parity = 0 if reverse else 1
    order = range(nblk - 1, -1, -1) if reverse else range(nblk)

    for i in range(nblk):
        blk = pl.ds(i * SCAN_BLOCK, SCAN_BLOCK)
        lf = lf_ref[blk, :]
        hi = lf.astype(BF16)
        mid = (lf - hi.astype(F32)).astype(BF16)
        c_ref[blk, :] = _dot(tri, hi) + _dot(tri, mid)

    def block(h, i):
        ls = pl.ds(h * HEAD_DIM, HEAD_DIM)
        base = i * SCAN_BLOCK
        blk = pl.ds(base, SCAN_BLOCK)
        c = c_ref[blk, ls]
        qf = q_ref[blk, ls].astype(F32)
        kf = k_ref[blk, ls].astype(F32)
        vb = v_ref[blk, ls]

        def ref_row(r, rows):
            return jnp.broadcast_to(c_ref[pl.ds(base + r, 1), ls], (rows, HEAD_DIM))

        mid_ref = jnp.concatenate([ref_row(g * sub + sub // 2, sub) for g in range(SCAN_BLOCK // sub)], 0)
        q_low = (qf * jnp.exp2(c - mid_ref)).astype(BF16)
        k_low = (kf * jnp.exp2(mid_ref - c)).astype(BF16)
        slot = (h * nblk + i) * nlevel
        xt_ref[slot] = k_low.T
        sc = jnp.where(code == 0, _dot(q_low, xt_ref[slot]), 0.0)

        later, earlier = (kf, qf) if reverse else (qf, kf)
        for level in range(LOW_LEVELS, nlevel):
            b = 1 << level
            expo, mixed = [], []
            for j in range(SCAN_BLOCK // (2 * b)):
                ref = ref_row((2 * j + 1) * b, b)
                even = c[2 * j * b:(2 * j + 1) * b]
                odd = c[(2 * j + 1) * b:(2 * j + 2) * b]
                expo += [even - ref, ref - odd] if reverse else [ref - even, odd - ref]
                mixed += [earlier[2 * j * b:(2 * j + 1) * b], later[(2 * j + 1) * b:(2 * j + 2) * b]]
            xs = (jnp.concatenate(mixed, 0) * jnp.exp2(jnp.concatenate(expo, 0))).astype(BF16)
            xt_ref[slot + level] = xs.T
            xt = xt_ref[slot + level]
            if b < 2 * sub:
                sc = jnp.where(code == level, _dot(xs, xt), sc)
                continue
            query = jnp.concatenate([xs[j * b:(j + 1) * b] for j in range(parity, SCAN_BLOCK // b, 2)], 0)
            p = _dot(query, xt)
            pieces = []
            for j in range(SCAN_BLOCK // b):
                rows = slice(j * b, (j + 1) * b)
                if j % 2 == parity:
                    pieces.append(jnp.where(code[rows] == level, p[(j // 2) * b:(j // 2 + 1) * b], sc[rows]))
                else:
                    pieces.append(sc[rows])
            sc = jnp.concatenate(pieces, 0)

        c_last = c_ref[pl.ds(base + last, 1), ls]
        qe = (qf * jnp.exp2(c)).astype(BF16)
        ke = (kf * jnp.exp2(c_last - c)).astype(BF16)
        return sc.astype(BF16), qe, ke, vb, jnp.exp2(c_last)

    def finish(h, i, sc, qe, ke, vb, decay):
        ls = pl.ds(h * HEAD_DIM, HEAD_DIM)
        blk = pl.ds(i * SCAN_BLOCK, SCAN_BLOCK)
        st = st_ref[h]
        out = _dot(sc, vb) + _dot_nt(qe, st.astype(BF16))
        o_ref[blk, ls] = out + acc_ref[blk, ls] if accumulate else out
        st_ref[h] = st * decay + _dot_tn(vb, ke)

    units = [(h, i) for i in order for h in range(heads)]
    pending = []
    for h, i in units:
        pending.append((h, i) + block(h, i))
        if len(pending) > SCAN_SKEW:
            finish(*pending.pop(0))
    for unit in pending:
        finish(*unit)


def _scan(reverse, batch, lf, q, k, v, acc=None):
    t, width = lf.shape
    rows = SCAN_ROWS
    nc = t // batch // rows
    tri, code = _scan_tables(reverse)
    if reverse:
        idx = lambda b, i: (b * nc + nc - 1 - i, 0)
    else:
        idx = lambda b, i: (b * nc + i, 0)
    row = pl.BlockSpec((rows, width), idx)
    heads = width // HEAD_DIM
    operands = (lf, q, k, v, tri, code) + (() if acc is None else (acc,))
    return pl.pallas_call(
        functools.partial(_scan_kernel, reverse, acc is not None),
        grid=(batch, nc),
        in_specs=[row, row, row, row, _resident(tri.shape), _resident(code.shape)] + ([] if acc is None else [row]),
        out_specs=row,
        out_shape=jax.ShapeDtypeStruct((t, width), F32),
        input_output_aliases={} if acc is None else {len(operands) - 1: 0},
        scratch_shapes=[pltpu.VMEM((heads, HEAD_DIM, HEAD_DIM), F32), pltpu.VMEM((rows, width), F32),
                        pltpu.VMEM((heads * (rows // SCAN_BLOCK) * (SCAN_BLOCK.bit_length() - 1), HEAD_DIM, SCAN_BLOCK),
                                   BF16)],
        compiler_params=pltpu.CompilerParams(dimension_semantics=("arbitrary", "arbitrary"),
                                             vmem_limit_bytes=VMEM_LIMIT),
        name="scan_bwd" if reverse else "scan_fwd",
    )(*operands)


def _mix_out_kernel(tiles_per_seq, x_ref, glu_ref, glu_prev_ref, glu_next_ref, o_ref, og_ref,
                    ga_ref, gb_ref, dww_ref, dwb_ref, lnw_ref, lnb_ref, pw_ref, hnw_ref, ow_ref, wout_ref,
                    xo_ref, ext_ref, shift_ref, act_ref, taps_ref):
    tm, cw = glu_ref.shape
    i = pl.program_id(0)
    first = (i % tiles_per_seq) == 0
    final = (i % tiles_per_seq) == tiles_per_seq - 1

    ext_ref[pl.ds(0, CONV_HALO), :] = jnp.where(first, 0.0, glu_prev_ref[...])
    ext_ref[pl.ds(CONV_HALO, tm), :] = glu_ref[...]
    ext_ref[pl.ds(CONV_HALO + tm, CONV_HALO), :] = jnp.where(final, 0.0, glu_next_ref[...])
    ext = ext_ref[...]
    for p in range(1, SUBLANES):
        shift_ref[p - 1] = pltpu.roll(ext, ext.shape[0] - p, 0)
    base = CONV_HALO - CONV_K // 2
    for j in range(CONV_K):
        taps_ref[j] = jnp.broadcast_to(dww_ref[pl.ds(j, 1), :], (SUBLANES, cw))

    def conv_rows(r):
        r0 = r * CONV_ROWS
        acc = jnp.broadcast_to(dwb_ref[...], (CONV_ROWS, cw))
        for j in range(CONV_K):
            whole, p = divmod(base + j, SUBLANES)
            src = ext_ref if p == 0 else shift_ref.at[p - 1]
            tap = jnp.concatenate([taps_ref[j]] * (CONV_ROWS // SUBLANES), 0)
            acc = acc + tap * src[pl.ds(r0 + whole * SUBLANES, CONV_ROWS), :]
        mu = jnp.mean(acc, axis=-1, keepdims=True)
        cen = acc - mu
        var = jnp.mean(cen * cen, axis=-1, keepdims=True)
        act_ref[pl.ds(r0, CONV_ROWS), :] = _silu(
            cen * lax.rsqrt(var + EPS) * lnw_ref[...] + lnb_ref[...]).astype(BF16)

    o = o_ref[...]
    parts = []
    for h in range(o.shape[1] // HEAD_DIM):
        oh = o[:, h * HEAD_DIM:(h + 1) * HEAD_DIM]
        parts.append(oh * lax.rsqrt(jnp.mean(oh * oh, axis=-1, keepdims=True) + EPS))
    of = jnp.concatenate(parts, axis=1) * hnw_ref[...]
    gated_b = gb_ref[...].astype(F32) * _dot((of * og_ref[...].astype(F32)).astype(BF16), ow_ref[...])

    halves = 2
    rows_per_half = tm // halves
    for half in range(halves):
        for r in range(half * rows_per_half // CONV_ROWS, (half + 1) * rows_per_half // CONV_ROWS):
            conv_rows(r)
        rows = pl.ds(half * rows_per_half, rows_per_half)
        a = _dot(act_ref[rows, :], pw_ref[...])
        y = ga_ref[rows, :].astype(F32) * a + gated_b[half * rows_per_half:(half + 1) * rows_per_half]
        xo_ref[rows, :] = x_ref[rows, :] + _dot(y.astype(BF16), wout_ref[...])


def _mix_out(seq, x2, glu, o, og, ga, gb, dw_w, dw_b, ln_w, ln_b, pw_b, hn_w, ow_b, wout_b):
    t, d = x2.shape
    cw = glu.shape[1]
    tm = TOKEN_TILE
    hb = tm // CONV_HALO
    nhalo = t // CONV_HALO
    row = lambda width: pl.BlockSpec((tm, width), lambda i: (i, 0))
    prev = pl.BlockSpec((CONV_HALO, cw), lambda i: (jnp.maximum(i * hb - 1, 0), 0))
    nxt = pl.BlockSpec((CONV_HALO, cw), lambda i: (jnp.minimum((i + 1) * hb, nhalo - 1), 0))
    weights = (dw_w, dw_b, ln_w, ln_b, pw_b, hn_w, ow_b, wout_b)
    return pl.pallas_call(
        functools.partial(_mix_out_kernel, seq // tm),
        grid=(t // tm,),
        in_specs=[row(d), row(cw), prev, nxt] + [row(d)] * 4 + [_resident(w.shape) for w in weights],
        out_specs=row(d),
        out_shape=jax.ShapeDtypeStruct((t, d), F32),
        scratch_shapes=[pltpu.VMEM((tm + 2 * CONV_HALO, cw), F32),
                        pltpu.VMEM((SUBLANES - 1, tm + 2 * CONV_HALO, cw), F32),
                        pltpu.VMEM((tm, cw), BF16),
                        pltpu.VMEM((CONV_K, SUBLANES, cw), F32)],
        compiler_params=pltpu.CompilerParams(dimension_semantics=("arbitrary",),
                                             vmem_limit_bytes=VMEM_LIMIT),
        name="mix_out",
    )(x2, glu, glu, glu, o, og, ga, gb, *weights)


def _ffn_kernel(tiles_per_seq, ff_chunks, apply_final, x_ref, x_prev_ref, x_next_ref, nw_ref, wup_ref,
                dww_ref, dwb_ref, wdn_ref, fnw_ref, xo_ref, hext_ref, gext_ref):
    tm, d = x_ref.shape
    dff = wdn_ref.shape[0]
    i = pl.program_id(0)
    first = (i % tiles_per_seq) == 0
    final = (i % tiles_per_seq) == tiles_per_seq - 1

    def norm(x):
        return (x * lax.rsqrt(jnp.mean(x * x, axis=-1, keepdims=True) + EPS) * nw_ref[...]).astype(BF16)

    x = x_ref[...]
    hext_ref[pl.ds(0, FFN_HALO), :] = norm(x_prev_ref[...])
    hext_ref[pl.ds(FFN_HALO, tm), :] = norm(x)
    hext_ref[pl.ds(FFN_HALO + tm, FFN_HALO), :] = norm(x_next_ref[...])

    rows = lax.broadcasted_iota(jnp.int32, (tm + 2 * FFN_HALO, 1), 0)
    outside = (first & (rows < FFN_HALO)) | (final & (rows >= FFN_HALO + tm))
    base = FFN_HALO - FFN_CONV_K // 2
    hidden = []
    lo = 0
    for fc in ff_chunks:
        gate = _dot(hext_ref[...], wup_ref[:, lo:lo + fc])
        gext_ref[:, lo:lo + fc] = jnp.where(outside, 0.0, gate)
        conv = jnp.broadcast_to(dwb_ref[:, lo:lo + fc], (tm, fc))
        for j in range(FFN_CONV_K):
            conv = conv + dww_ref[pl.ds(j, 1), lo:lo + fc] * gext_ref[pl.ds(base + j, tm), lo:lo + fc]
        val = _dot(hext_ref[pl.ds(FFN_HALO, tm), :], wup_ref[:, dff + lo:dff + lo + fc])
        hidden.append((_silu(conv) * val).astype(BF16))
        lo += fc
    acc = x
    lo = 0
    for fc, u in zip(ff_chunks, hidden):
        acc = acc + _dot(u, wdn_ref[lo:lo + fc, :])
        lo += fc
    if apply_final:
        acc = acc * lax.rsqrt(jnp.mean(acc * acc, axis=-1, keepdims=True) + EPS) * fnw_ref[...]
    xo_ref[...] = acc


def _ffn_chunks(dff):
    tiles = dff // MXU_COLUMNS
    assert tiles * MXU_COLUMNS == dff
    first = (tiles + 1) // 2 * MXU_COLUMNS
    return (first, dff - first)


def _ffn(seq, apply_final, x2, norm_w, wup_b, dw_w, dw_b, wdn_b, final_w):
    t, d = x2.shape
    dff = wdn_b.shape[0]
    tm = TOKEN_TILE
    ff_chunks = _ffn_chunks(dff)
    hb = tm // FFN_HALO
    nhalo = t // FFN_HALO
    row = pl.BlockSpec((tm, d), lambda i: (i, 0))
    prev = pl.BlockSpec((FFN_HALO, d), lambda i: (jnp.maximum(i * hb - 1, 0), 0))
    nxt = pl.BlockSpec((FFN_HALO, d), lambda i: (jnp.minimum((i + 1) * hb, nhalo - 1), 0))
    weights = (norm_w, wup_b, dw_w, dw_b, wdn_b, final_w)
    return pl.pallas_call(
        functools.partial(_ffn_kernel, seq // tm, ff_chunks, apply_final),
        grid=(t // tm,),
        in_specs=[row, prev, nxt] + [_resident(w.shape) for w in weights],
        out_specs=row,
        out_shape=jax.ShapeDtypeStruct((t, d), F32),
        scratch_shapes=[pltpu.VMEM((tm + 2 * FFN_HALO, d), BF16),
                        pltpu.VMEM((tm + 2 * FFN_HALO, dff), F32)],
        compiler_params=pltpu.CompilerParams(dimension_semantics=("arbitrary",),
                                             vmem_limit_bytes=VMEM_LIMIT),
        name="ffn",
    )(x2, x2, x2, *weights)


def kernel(x, attn_norm_w, w_in, conv_dw_w, conv_dw_b, conv_ln_w, conv_ln_b, conv_pw_w, lb_logits,
           hgrn_norm_w, hgrn_o_w, w_out, ffn_norm_w, ffn_w_up, ffn_dw_w, ffn_dw_b, ffn_w_down, final_norm_w):
    batch, seq, d = x.shape
    depth = w_in.shape[0]
    x2 = x.reshape(batch * seq, d)
    lbl = lb_logits.astype(F32).reshape(depth * 2, -1)
    vec = lambda w: w.reshape(1, -1).astype(F32)
    for layer in range(depth):
        glu, q, v, lf_f, k_f, lf_b, k_b, og, ga, gb = _in_proj(
            layer, depth, x2, vec(attn_norm_w[layer]), w_in[layer].astype(BF16), lbl)
        o = _scan(False, batch, lf_f, q, k_f, v)
        o = _scan(True, batch, lf_b, q, k_b, v, acc=o)
        x2 = _mix_out(seq, x2, glu, o, og, ga, gb,
                      conv_dw_w[layer], vec(conv_dw_b[layer]), vec(conv_ln_w[layer]), vec(conv_ln_b[layer]),
                      conv_pw_w[layer].astype(BF16), vec(hgrn_norm_w[layer]),
                      hgrn_o_w[layer].astype(BF16), w_out[layer].astype(BF16))
        x2 = _ffn(seq, layer == depth - 1, x2, vec(ffn_norm_w[layer]), ffn_w_up[layer].astype(BF16),
                  ffn_dw_w[layer], vec(ffn_dw_b[layer]), ffn_w_down[layer].astype(BF16), vec(final_norm_w))
    return x2.reshape(batch, seq, d)
```

```python
import functools

import numpy as np
import jax
import jax.numpy as jnp
from jax import lax
from jax.experimental import pallas as pl
from jax.experimental.pallas import tpu as pltpu

F32 = jnp.float32
BF16 = jnp.bfloat16

EPS = 1e-6
MIN_FORGET = 1e-6
HEAD_DIM = 128
CONV_K = 31
FFN_CONV_K = 3
CONV_HALO = 16
SUBLANES = 8
CONV_ROWS = 32
MXU_COLUMNS = 256
FFN_HALO = 8
SCAN_BLOCK = 128
SCAN_ROWS = 512
SCAN_SKEW = 1
LOW_LEVELS = 3
assert LOW_LEVELS > 1
TOKEN_TILE = 512
VMEM_LIMIT = 56 * 1024 * 1024


def _dot(a, b):
    return jnp.dot(a, b, preferred_element_type=F32)


def _dot_nt(a, b):
    return lax.dot_general(a, b, (((1,), (1,)), ((), ())), preferred_element_type=F32)


def _dot_tn(a, b):
    return lax.dot_general(a, b, (((0,), (0,)), ((), ())), preferred_element_type=F32)


def _sigmoid(x):
    return 0.5 * jnp.tanh(0.5 * x) + 0.5


def _silu(x):
    return x * _sigmoid(x)


def _resident(shape):
    nd = len(shape)
    return pl.BlockSpec(shape, lambda *_: (0,) * nd, pipeline_mode=pl.Buffered(1))


def _in_proj_kernel(layer, depth, x_ref, nw_ref, w_ref, lbl_ref,
                    glu_ref, q_ref, v_ref, lff_ref, kf_ref, lfb_ref, kb_ref, og_ref, ga_ref, gb_ref):
    d = x_ref.shape[1]
    x = x_ref[...]
    h = x * lax.rsqrt(jnp.mean(x * x, axis=-1, keepdims=True) + EPS) * nw_ref[...]
    hb = h.astype(BF16)

    def sec(lo, width):
        return _dot(hb, w_ref[:, lo:lo + width])

    def sigmoid_h(xh):
        return 0.5 + 0.5 * jnp.tanh(xh)

    def silu_h(xh):
        return xh + xh * jnp.tanh(xh)

    half = d // 2
    ah = sec(0, half)
    glu_ref[...] = ah + ah * jnp.tanh(sec(half, half))
    q_ref[...] = (silu_h(sec(d, d)) * (HEAD_DIM ** -0.5)).astype(q_ref.dtype)

    def lower_bound(direction):
        rows = [lbl_ref[2 * i + direction:2 * i + direction + 1, :] for i in range(depth)]
        m = functools.reduce(jnp.maximum, rows)
        e = [jnp.exp(r - m) for r in rows]
        den = functools.reduce(lambda a, b: a + b, e)
        p = [ei / den for ei in e]
        cum = functools.reduce(lambda a, b: a + b, p[:layer + 1])
        return cum - p[0]

    def forget(zh, lb, lf_ref, k_ref):
        a = 0.5 + 0.5 * lb
        b = 0.5 - 0.5 * lb
        bt = b * jnp.tanh(zh)
        lf_ref[...] = jnp.log2(jnp.clip(a + bt, MIN_FORGET, 1.0))
        k_ref[...] = (b - bt).astype(k_ref.dtype)

    forget(sec(3 * d, d), lower_bound(0), lff_ref, kf_ref)
    forget(sec(4 * d, d), lower_bound(1), lfb_ref, kb_ref)
    og_ref[...] = silu_h(sec(5 * d, d)).astype(og_ref.dtype)
    ga_ref[...] = sigmoid_h(sec(6 * d, d)).astype(ga_ref.dtype)
    gb_ref[...] = sigmoid_h(sec(7 * d, d)).astype(gb_ref.dtype)
    v_ref[...] = sec(2 * d, d).astype(v_ref.dtype)


def _in_proj(layer, depth, x2, norm_w, w_in_b, lbl):
    t, d = x2.shape
    tm = TOKEN_TILE
    row = lambda width: pl.BlockSpec((tm, width), lambda i: (i, 0))
    act = lambda dtype, width=d: jax.ShapeDtypeStruct((t, width), dtype)
    return pl.pallas_call(
        functools.partial(_in_proj_kernel, layer, depth),
        grid=(t // tm,),
        in_specs=[row(d), _resident((1, d)), _resident(w_in_b.shape), _resident(lbl.shape)],
        out_specs=[row(d // 2)] + [row(d)] * 9,
        out_shape=[act(F32, d // 2), act(BF16), act(BF16), act(F32), act(BF16), act(F32), act(BF16),
                   act(BF16), act(BF16), act(BF16)],
        compiler_params=pltpu.CompilerParams(dimension_semantics=("arbitrary",),
                                             vmem_limit_bytes=VMEM_LIMIT),
        name="in_proj",
    )(x2, norm_w, w_in_b, lbl)


def _scan_tables(reverse):
    b = np.arange(SCAN_BLOCK)
    tri = (b[None, :] >= b[:, None]) if reverse else (b[None, :] <= b[:, None])
    xor = b[:, None] ^ b[None, :]
    level = np.floor(np.log2(np.maximum(xor, 1))).astype(np.int32)
    before = (b[None, :] > b[:, None]) if reverse else (b[None, :] < b[:, None])
    code = np.where(before | (xor == 0), np.where(level < LOW_LEVELS, 0, level), -2).astype(np.int32)
    return jnp.asarray(tri, BF16), jnp.asarray(code)


def _scan_kernel(reverse, accumulate, lf_ref, q_ref, k_ref, v_ref, tri_ref, code_ref, *rest):
    acc_ref, (o_ref, st_ref, c_ref, xt_ref) = (rest[0], rest[1:]) if accumulate else (None, rest)
    rows_per_step, width = lf_ref.shape
    nblk = rows_per_step // SCAN_BLOCK
    nlevel = SCAN_BLOCK.bit_length() - 1
    heads = width // HEAD_DIM
    sub = SUBLANES

    @pl.when(pl.program_id(1) == 0)
    def _():
        st_ref[...] = jnp.zeros_like(st_ref)

    tri = tri_ref[...]
    code = code_ref[...]
    last = 0 if reverse else SCAN_BLOCK - 1
    parity = 0 if reverse else 1
    order = range(nblk - 1, -1, -1) if reverse else range(nblk)

    for i in range(nblk):
        blk = pl.ds(i * SCAN_BLOCK, SCAN_BLOCK)
        lf = lf_ref[blk, :]
        hi = lf.astype(BF16)
        mid = (lf - hi.astype(F32)).astype(BF16)
        c_ref[blk, :] = _dot(tri, hi) + _dot(tri, mid)

    def block(h, i):
        ls = pl.ds(h * HEAD_DIM, HEAD_DIM)
        base = i * SCAN_BLOCK
        blk = pl.ds(base, SCAN_BLOCK)
        c = c_ref[blk, ls]
        qf = q_ref[blk, ls].astype(F32)
        kf = k_ref[blk, ls].astype(F32)
        vb = v_ref[blk, ls]

        def ref_row(r, rows):
            return jnp.broadcast_to(c_ref[pl.ds(base + r, 1), ls], (rows, HEAD_DIM))

        mid_ref = jnp.concatenate([ref_row(g * sub + sub // 2, sub) for g in range(SCAN_BLOCK // sub)], 0)
        q_low = (qf * jnp.exp2(c - mid_ref)).astype(BF16)
        k_low = (kf * jnp.exp2(mid_ref - c)).astype(BF16)
        slot = (h * nblk + i) * nlevel
        xt_ref[slot] = k_low.T
        sc = jnp.where(code == 0, _dot(q_low, xt_ref[slot]), 0.0)

        later, earlier = (kf, qf) if reverse else (qf, kf)
        for level in range(LOW_LEVELS, nlevel):
            b = 1 << level
            expo, mixed = [], []
            for j in range(SCAN_BLOCK // (2 * b)):
                ref = ref_row((2 * j + 1) * b, b)
                even = c[2 * j * b:(2 * j + 1) * b]
                odd = c[(2 * j + 1) * b:(2 * j + 2) * b]
                expo += [even - ref, ref - odd] if reverse else [ref - even, odd - ref]
                mixed += [earlier[2 * j * b:(2 * j + 1) * b], later[(2 * j + 1) * b:(2 * j + 2) * b]]
            xs = (jnp.concatenate(mixed, 0) * jnp.exp2(jnp.concatenate(expo, 0))).astype(BF16)
            xt_ref[slot + level] = xs.T
            xt = xt_ref[slot + level]
            if b < 2 * sub:
                sc = jnp.where(code == level, _dot(xs, xt), sc)
                continue
            query = jnp.concatenate([xs[j * b:(j + 1) * b] for j in range(parity, SCAN_BLOCK // b, 2)], 0)
            p = _dot(query, xt)
            pieces = []
            for j in range(SCAN_BLOCK // b):
                rows = slice(j * b, (j + 1) * b)
                if j % 2 == parity:
                    pieces.append(jnp.where(code[rows] == level, p[(j // 2) * b:(j // 2 + 1) * b], sc[rows]))
                else:
                    pieces.append(sc[rows])
            sc = jnp.concatenate(pieces, 0)

        c_last = c_ref[pl.ds(base + last, 1), ls]
        qe = (qf * jnp.exp2(c)).astype(BF16)
        ke = (kf * jnp.exp2(c_last - c)).astype(BF16)
        return sc.astype(BF16), qe, ke, vb, jnp.exp2(c_last)

    def finish(h, i, sc, qe, ke, vb, decay):
        ls = pl.ds(h * HEAD_DIM, HEAD_DIM)
        blk = pl.ds(i * SCAN_BLOCK, SCAN_BLOCK)
        st = st_ref[h]
        plane = (h * nblk + i) * nlevel + 1
        xt_ref[plane] = st.astype(BF16).T
        out = _dot(sc, vb) + _dot(qe, xt_ref[plane])
        o_ref[blk, ls] = out + acc_ref[blk, ls] if accumulate else out
        st_ref[h] = st * decay + _dot_tn(vb, ke)

    units = [(h, i) for i in order for h in range(heads)]
    pending = []
    for h, i in units:
        pending.append((h, i) + block(h, i))
        if len(pending) > SCAN_SKEW:
            finish(*pending.pop(0))
    for unit in pending:
        finish(*unit)


def _scan(reverse, batch, lf, q, k, v, acc=None):
    t, width = lf.shape
    rows = SCAN_ROWS
    nc = t // batch // rows
    tri, code = _scan_tables(reverse)
    if reverse:
        idx = lambda b, i: (b * nc + nc - 1 - i, 0)
    else:
        idx = lambda b, i: (b * nc + i, 0)
    row = pl.BlockSpec((rows, width), idx)
    heads = width // HEAD_DIM
    operands = (lf, q, k, v, tri, code) + (() if acc is None else (acc,))
    return pl.pallas_call(
        functools.partial(_scan_kernel, reverse, acc is not None),
        grid=(batch, nc),
        in_specs=[row, row, row, row, _resident(tri.shape), _resident(code.shape)] + ([] if acc is None else [row]),
        out_specs=row,
        out_shape=jax.ShapeDtypeStruct((t, width), F32),
        input_output_aliases={} if acc is None else {len(operands) - 1: 0},
        scratch_shapes=[pltpu.VMEM((heads, HEAD_DIM, HEAD_DIM), F32), pltpu.VMEM((rows, width), F32),
                        pltpu.VMEM((heads * (rows // SCAN_BLOCK) * (SCAN_BLOCK.bit_length() - 1), HEAD_DIM, SCAN_BLOCK),
                                   BF16)],
        compiler_params=pltpu.CompilerParams(dimension_semantics=("arbitrary", "arbitrary"),
                                             vmem_limit_bytes=VMEM_LIMIT),
        name="scan_bwd" if reverse else "scan_fwd",
    )(*operands)


def _mix_out_kernel(tiles_per_seq, x_ref, glu_ref, glu_prev_ref, glu_next_ref, o_ref, og_ref,
                    ga_ref, gb_ref, dww_ref, dwb_ref, lnw_ref, lnb_ref, pw_ref, hnw_ref, ow_ref, wout_ref,
                    xo_ref, ext_ref, shift_ref, act_ref, taps_ref):
    tm, cw = glu_ref.shape
    i = pl.program_id(0)
    first = (i % tiles_per_seq) == 0
    final = (i % tiles_per_seq) == tiles_per_seq - 1

    ext_ref[pl.ds(0, CONV_HALO), :] = jnp.where(first, 0.0, glu_prev_ref[...])
    ext_ref[pl.ds(CONV_HALO, tm), :] = glu_ref[...]
    ext_ref[pl.ds(CONV_HALO + tm, CONV_HALO), :] = jnp.where(final, 0.0, glu_next_ref[...])
    ext = ext_ref[...]
    for p in range(1, SUBLANES):
        shift_ref[p - 1] = pltpu.roll(ext, ext.shape[0] - p, 0)
    base = CONV_HALO - CONV_K // 2
    for j in range(CONV_K):
        taps_ref[j] = jnp.broadcast_to(dww_ref[pl.ds(j, 1), :], (SUBLANES, cw))

    def conv_rows(r):
        r0 = r * CONV_ROWS
        acc = jnp.broadcast_to(dwb_ref[...], (CONV_ROWS, cw))
        for j in range(CONV_K):
            whole, p = divmod(base + j, SUBLANES)
            src = ext_ref if p == 0 else shift_ref.at[p - 1]
            tap = jnp.concatenate([taps_ref[j]] * (CONV_ROWS // SUBLANES), 0)
            acc = acc + tap * src[pl.ds(r0 + whole * SUBLANES, CONV_ROWS), :]
        mu = jnp.mean(acc, axis=-1, keepdims=True)
        cen = acc - mu
        var = jnp.mean(cen * cen, axis=-1, keepdims=True)
        act_ref[pl.ds(r0, CONV_ROWS), :] = _silu(
            cen * lax.rsqrt(var + EPS) * lnw_ref[...] + lnb_ref[...]).astype(BF16)

    o = o_ref[...]
    parts = []
    for h in range(o.shape[1] // HEAD_DIM):
        oh = o[:, h * HEAD_DIM:(h + 1) * HEAD_DIM]
        parts.append(oh * lax.rsqrt(jnp.mean(oh * oh, axis=-1, keepdims=True) + EPS))
    of = jnp.concatenate(parts, axis=1) * hnw_ref[...]
    gated_b = gb_ref[...].astype(F32) * _dot((of * og_ref[...].astype(F32)).astype(BF16), ow_ref[...])

    halves = 2
    rows_per_half = tm // halves
    for half in range(halves):
        for r in range(half * rows_per_half // CONV_ROWS, (half + 1) * rows_per_half // CONV_ROWS):
            conv_rows(r)
        rows = pl.ds(half * rows_per_half, rows_per_half)
        a = _dot(act_ref[rows, :], pw_ref[...])
        y = ga_ref[rows, :].astype(F32) * a + gated_b[half * rows_per_half:(half + 1) * rows_per_half]
        xo_ref[rows, :] = x_ref[rows, :] + _dot(y.astype(BF16), wout_ref[...])


def _mix_out(seq, x2, glu, o, og, ga, gb, dw_w, dw_b, ln_w, ln_b, pw_b, hn_w, ow_b, wout_b):
    t, d = x2.shape
    cw = glu.shape[1]
    tm = TOKEN_TILE
    hb = tm // CONV_HALO
    nhalo = t // CONV_HALO
    row = lambda width: pl.BlockSpec((tm, width), lambda i: (i, 0))
    prev = pl.BlockSpec((CONV_HALO, cw), lambda i: (jnp.maximum(i * hb - 1, 0), 0))
    nxt = pl.BlockSpec((CONV_HALO, cw), lambda i: (jnp.minimum((i + 1) * hb, nhalo - 1), 0))
    weights = (dw_w, dw_b, ln_w, ln_b, pw_b, hn_w, ow_b, wout_b)
    return pl.pallas_call(
        functools.partial(_mix_out_kernel, seq // tm),
        grid=(t // tm,),
        in_specs=[row(d), row(cw), prev, nxt] + [row(d)] * 4 + [_resident(w.shape) for w in weights],
        out_specs=row(d),
        out_shape=jax.ShapeDtypeStruct((t, d), F32),
        scratch_shapes=[pltpu.VMEM((tm + 2 * CONV_HALO, cw), F32),
                        pltpu.VMEM((SUBLANES - 1, tm + 2 * CONV_HALO, cw), F32),
                        pltpu.VMEM((tm, cw), BF16),
                        pltpu.VMEM((CONV_K, SUBLANES, cw), F32)],
        compiler_params=pltpu.CompilerParams(dimension_semantics=("arbitrary",),
                                             vmem_limit_bytes=VMEM_LIMIT),
        name="mix_out",
    )(x2, glu, glu, glu, o, og, ga, gb, *weights)


def _ffn_kernel(tiles_per_seq, ff_chunks, apply_final, x_ref, x_prev_ref, x_next_ref, nw_ref, wup_ref,
                dww_ref, dwb_ref, wdn_ref, fnw_ref, xo_ref, hext_ref, gext_ref):
    tm, d = x_ref.shape
    dff = wdn_ref.shape[0]
    i = pl.program_id(0)
    first = (i % tiles_per_seq) == 0
    final = (i % tiles_per_seq) == tiles_per_seq - 1

    def norm(x):
        return (x * lax.rsqrt(jnp.mean(x * x, axis=-1, keepdims=True) + EPS) * nw_ref[...]).astype(BF16)

    x = x_ref[...]
    hext_ref[pl.ds(0, FFN_HALO), :] = norm(x_prev_ref[...])
    hext_ref[pl.ds(FFN_HALO, tm), :] = norm(x)
    hext_ref[pl.ds(FFN_HALO + tm, FFN_HALO), :] = norm(x_next_ref[...])

    rows = lax.broadcasted_iota(jnp.int32, (tm + 2 * FFN_HALO, 1), 0)
    outside = (first & (rows < FFN_HALO)) | (final & (rows >= FFN_HALO + tm))
    base = FFN_HALO - FFN_CONV_K // 2
    hidden = []
    lo = 0
    for fc in ff_chunks:
        gate = _dot(hext_ref[...], wup_ref[:, lo:lo + fc])
        gext_ref[:, lo:lo + fc] = jnp.where(outside, 0.0, gate)
        conv = jnp.broadcast_to(dwb_ref[:, lo:lo + fc], (tm, fc))
        for j in range(FFN_CONV_K):
            conv = conv + dww_ref[pl.ds(j, 1), lo:lo + fc] * gext_ref[pl.ds(base + j, tm), lo:lo + fc]
        val = _dot(hext_ref[pl.ds(FFN_HALO, tm), :], wup_ref[:, dff + lo:dff + lo + fc])
        hidden.append((_silu(conv) * val).astype(BF16))
        lo += fc
    acc = x
    lo = 0
    for fc, u in zip(ff_chunks, hidden):
        acc = acc + _dot(u, wdn_ref[lo:lo + fc, :])
        lo += fc
    if apply_final:
        acc = acc * lax.rsqrt(jnp.mean(acc * acc, axis=-1, keepdims=True) + EPS) * fnw_ref[...]
    xo_ref[...] = acc


def _ffn_chunks(dff):
    tiles = dff // MXU_COLUMNS
    assert tiles * MXU_COLUMNS == dff
    first = (tiles + 1) // 2 * MXU_COLUMNS
    return (first, dff - first)


def _ffn(seq, apply_final, x2, norm_w, wup_b, dw_w, dw_b, wdn_b, final_w):
    t, d = x2.shape
    dff = wdn_b.shape[0]
    tm = TOKEN_TILE
    ff_chunks = _ffn_chunks(dff)
    hb = tm // FFN_HALO
    nhalo = t // FFN_HALO
    row = pl.BlockSpec((tm, d), lambda i: (i, 0))
    prev = pl.BlockSpec((FFN_HALO, d), lambda i: (jnp.maximum(i * hb - 1, 0), 0))
    nxt = pl.BlockSpec((FFN_HALO, d), lambda i: (jnp.minimum((i + 1) * hb, nhalo - 1), 0))
    weights = (norm_w, wup_b, dw_w, dw_b, wdn_b, final_w)
    return pl.pallas_call(
        functools.partial(_ffn_kernel, seq // tm, ff_chunks, apply_final),
        grid=(t // tm,),
        in_specs=[row, prev, nxt] + [_resident(w.shape) for w in weights],
        out_specs=row,
        out_shape=jax.ShapeDtypeStruct((t, d), F32),
        scratch_shapes=[pltpu.VMEM((tm + 2 * FFN_HALO, d), BF16),
                        pltpu.VMEM((tm + 2 * FFN_HALO, dff), F32)],
        compiler_params=pltpu.CompilerParams(dimension_semantics=("arbitrary",),
                                             vmem_limit_bytes=VMEM_LIMIT),
        name="ffn",
    )(x2, x2, x2, *weights)


def kernel(x, attn_norm_w, w_in, conv_dw_w, conv_dw_b, conv_ln_w, conv_ln_b, conv_pw_w, lb_logits,
           hgrn_norm_w, hgrn_o_w, w_out, ffn_norm_w, ffn_w_up, ffn_dw_w, ffn_dw_b, ffn_w_down, final_norm_w):
    batch, seq, d = x.shape
    depth = w_in.shape[0]
    x2 = x.reshape(batch * seq, d)
    lbl = lb_logits.astype(F32).reshape(depth * 2, -1)
    vec = lambda w: w.reshape(1, -1).astype(F32)
    col = jnp.arange(w_in.shape[2])
    in_scale = jnp.where((col >= 2 * d) & (col < 3 * d), 1.0, 0.5).astype(F32)
    for layer in range(depth):
        glu, q, v, lf_f, k_f, lf_b, k_b, og, ga, gb = _in_proj(
            layer, depth, x2, vec(attn_norm_w[layer]), (w_in[layer] * in_scale).astype(BF16), lbl)
        o = _scan(False, batch, lf_f, q, k_f, v)
        o = _scan(True, batch, lf_b, q, k_b, v, acc=o)
        x2 = _mix_out(seq, x2, glu, o, og, ga, gb,
                      conv_dw_w[layer], vec(conv_dw_b[layer]), vec(conv_ln_w[layer]), vec(conv_ln_b[layer]),
                      conv_pw_w[layer].astype(BF16), vec(hgrn_norm_w[layer]),
                      hgrn_o_w[layer].astype(BF16), w_out[layer].astype(BF16))
        x2 = _ffn(seq, layer == depth - 1, x2, vec(ffn_norm_w[layer]), ffn_w_up[layer].astype(BF16),
                  ffn_dw_w[layer], vec(ffn_dw_b[layer]), ffn_w_down[layer].astype(BF16), vec(final_norm_w))
    return x2.reshape(batch, seq, d)
```

```python
import functools

import numpy as np
import jax
import jax.numpy as jnp
from jax import lax
from jax.experimental import pallas as pl
from jax.experimental.pallas import tpu as pltpu

F32 = jnp.float32
BF16 = jnp.bfloat16

EPS = 1e-6
MIN_FORGET = 1e-6
HEAD_DIM = 128
CONV_K = 31
FFN_CONV_K = 3
CONV_HALO = 16
SUBLANES = 8
CONV_ROWS = 32
MXU_COLUMNS = 256
FFN_HALO = 8
SCAN_BLOCK = 128
SCAN_ROWS = 512
SCAN_SKEW = 1
LOW_LEVELS = 3
TOKEN_TILE = 512
FFN_TILE = 1024
VMEM_LIMIT = 56 * 1024 * 1024


def _dot(a, b):
    return jnp.dot(a, b, preferred_element_type=F32)


def _dot_nt(a, b):
    return lax.dot_general(a, b, (((1,), (1,)), ((), ())), preferred_element_type=F32)


def _dot_tn(a, b):
    return lax.dot_general(a, b, (((0,), (0,)), ((), ())), preferred_element_type=F32)


def _sigmoid(x):
    return 0.5 * jnp.tanh(0.5 * x) + 0.5


def _silu(x):
    return x * _sigmoid(x)


def _resident(shape):
    nd = len(shape)
    return pl.BlockSpec(shape, lambda *_: (0,) * nd, pipeline_mode=pl.Buffered(1))


def _in_proj_kernel(layer, depth, x_ref, nw_ref, w_ref, lbl_ref,
                    glu_ref, q_ref, v_ref, lff_ref, kf_ref, lfb_ref, kb_ref, og_ref, ga_ref, gb_ref):
    d = x_ref.shape[1]
    x = x_ref[...]
    h = x * lax.rsqrt(jnp.mean(x * x, axis=-1, keepdims=True) + EPS) * nw_ref[...]
    hb = h.astype(BF16)

    def sec(lo, width):
        return _dot(hb, w_ref[:, lo:lo + width])

    def sigmoid_h(xh):
        return 0.5 + 0.5 * jnp.tanh(xh)

    def silu_h(xh):
        return xh + xh * jnp.tanh(xh)

    half = d // 2
    ah = sec(0, half)
    glu_ref[...] = ah + ah * jnp.tanh(sec(half, half))
    q_ref[...] = (silu_h(sec(d, d)) * (HEAD_DIM ** -0.5)).astype(q_ref.dtype)

    def lower_bound(direction):
        rows = [lbl_ref[2 * i + direction:2 * i + direction + 1, :] for i in range(depth)]
        m = functools.reduce(jnp.maximum, rows)
        e = [jnp.exp(r - m) for r in rows]
        den = functools.reduce(lambda a, b: a + b, e)
        p = [ei / den for ei in e]
        cum = functools.reduce(lambda a, b: a + b, p[:layer + 1])
        return cum - p[0]

    def forget(zh, lb, lf_ref, k_ref):
        a = 0.5 + 0.5 * lb
        b = 0.5 - 0.5 * lb
        bt = b * jnp.tanh(zh)
        lf_ref[...] = jnp.log2(jnp.clip(a + bt, MIN_FORGET, 1.0))
        k_ref[...] = (b - bt).astype(k_ref.dtype)

    forget(sec(3 * d, d), lower_bound(0), lff_ref, kf_ref)
    forget(sec(4 * d, d), lower_bound(1), lfb_ref, kb_ref)
    og_ref[...] = silu_h(sec(5 * d, d)).astype(og_ref.dtype)
    ga_ref[...] = sigmoid_h(sec(6 * d, d)).astype(ga_ref.dtype)
    gb_ref[...] = sigmoid_h(sec(7 * d, d)).astype(gb_ref.dtype)
    v_ref[...] = sec(2 * d, d).astype(v_ref.dtype)


def _in_proj(layer, depth, x2, norm_w, w_in_b, lbl):
    t, d = x2.shape
    tm = TOKEN_TILE
    row = lambda width: pl.BlockSpec((tm, width), lambda i: (i, 0))
    act = lambda dtype, width=d: jax.ShapeDtypeStruct((t, width), dtype)
    return pl.pallas_call(
        functools.partial(_in_proj_kernel, layer, depth),
        grid=(t // tm,),
        in_specs=[row(d), _resident((1, d)), _resident(w_in_b.shape), _resident(lbl.shape)],
        out_specs=[row(d // 2)] + [row(d)] * 9,
        out_shape=[act(F32, d // 2), act(BF16), act(BF16), act(F32), act(BF16), act(F32), act(BF16),
                   act(BF16), act(BF16), act(BF16)],
        compiler_params=pltpu.CompilerParams(dimension_semantics=("arbitrary",),
                                             vmem_limit_bytes=VMEM_LIMIT),
        name="in_proj",
    )(x2, norm_w, w_in_b, lbl)


def _scan_tables(reverse):
    b = np.arange(SCAN_BLOCK)
    tri = (b[None, :] >= b[:, None]) if reverse else (b[None, :] <= b[:, None])
    xor = b[:, None] ^ b[None, :]
    level = np.floor(np.log2(np.maximum(xor, 1))).astype(np.int32)
    before = (b[None, :] > b[:, None]) if reverse else (b[None, :] < b[:, None])
    code = np.where(before | (xor == 0), np.where(level < LOW_LEVELS, 0, level), -2).astype(np.int32)
    return jnp.asarray(tri, BF16), jnp.asarray(code)


def _scan_kernel(reverse, accumulate, lf_ref, q_ref, k_ref, v_ref, tri_ref, code_ref, *rest):
    acc_ref, (o_ref, st_ref, c_ref, xt_ref) = (rest[0], rest[1:]) if accumulate else (None, rest)
    rows_per_step, width = lf_ref.shape
    nblk = rows_per_step // SCAN_BLOCK
    nlevel = SCAN_BLOCK.bit_length() - 1
    heads = width // HEAD_DIM
    sub = SUBLANES

    @pl.when(pl.program_id(1) == 0)
    def _():
        st_ref[...] = jnp.zeros_like(st_ref)

    tri = tri_ref[...]
    code = code_ref[...]
    last = 0 if reverse else SCAN_BLOCK - 1
    parity = 0 if reverse else 1
    order = range(nblk - 1, -1, -1) if reverse else range(nblk)

    for i in range(nblk):
        blk = pl.ds(i * SCAN_BLOCK, SCAN_BLOCK)
        lf = lf_ref[blk, :]
        hi = lf.astype(BF16)
        mid = (lf - hi.astype(F32)).astype(BF16)
        c_ref[blk, :] = _dot(tri, hi) + _dot(tri, mid)

    def block(h, i):
        ls = pl.ds(h * HEAD_DIM, HEAD_DIM)
        base = i * SCAN_BLOCK
        blk = pl.ds(base, SCAN_BLOCK)
        c = c_ref[blk, ls]
        qf = q_ref[blk, ls].astype(F32)
        kf = k_ref[blk, ls].astype(F32)
        vb = v_ref[blk, ls]

        def ref_row(r, rows):
            return jnp.broadcast_to(c_ref[pl.ds(base + r, 1), ls], (rows, HEAD_DIM))

        mid_ref = jnp.concatenate([ref_row(g * sub + sub // 2, sub) for g in range(SCAN_BLOCK // sub)], 0)
        q_low = (qf * jnp.exp2(c - mid_ref)).astype(BF16)
        k_low = (kf * jnp.exp2(mid_ref - c)).astype(BF16)
        slot = (h * nblk + i) * nlevel
        xt_ref[slot] = k_low.T
        sc = jnp.where(code == 0, _dot(q_low, xt_ref[slot]), 0.0)

        later, earlier = (kf, qf) if reverse else (qf, kf)
        for level in range(LOW_LEVELS, nlevel):
            b = 1 << level
            expo, mixed = [], []
            for j in range(SCAN_BLOCK // (2 * b)):
                ref = ref_row((2 * j + 1) * b, b)
                even = c[2 * j * b:(2 * j + 1) * b]
                odd = c[(2 * j + 1) * b:(2 * j + 2) * b]
                expo += [even - ref, ref - odd] if reverse else [ref - even, odd - ref]
                mixed += [earlier[2 * j * b:(2 * j + 1) * b], later[(2 * j + 1) * b:(2 * j + 2) * b]]
            xs = (jnp.concatenate(mixed, 0) * jnp.exp2(jnp.concatenate(expo, 0))).astype(BF16)
            xt_ref[slot + level] = xs.T
            xt = xt_ref[slot + level]
            if b < 2 * sub:
                sc = jnp.where(code == level, _dot(xs, xt), sc)
                continue
            query = jnp.concatenate([xs[j * b:(j + 1) * b] for j in range(parity, SCAN_BLOCK // b, 2)], 0)
            p = _dot(query, xt)
            pieces = []
            for j in range(SCAN_BLOCK // b):
                rows = slice(j * b, (j + 1) * b)
                if j % 2 == parity:
                    pieces.append(jnp.where(code[rows] == level, p[(j // 2) * b:(j // 2 + 1) * b], sc[rows]))
                else:
                    pieces.append(sc[rows])
            sc = jnp.concatenate(pieces, 0)

        c_last = c_ref[pl.ds(base + last, 1), ls]
        qe = (qf * jnp.exp2(c)).astype(BF16)
        ke = (kf * jnp.exp2(c_last - c)).astype(BF16)
        return sc.astype(BF16), qe, ke, vb, jnp.exp2(c_last)

    def finish(h, i, sc, qe, ke, vb, decay):
        ls = pl.ds(h * HEAD_DIM, HEAD_DIM)
        blk = pl.ds(i * SCAN_BLOCK, SCAN_BLOCK)
        st = st_ref[h]
        out = _dot(sc, vb) + _dot_nt(qe, st.astype(BF16))
        o_ref[blk, ls] = out + acc_ref[blk, ls] if accumulate else out
        st_ref[h] = st * decay + _dot_tn(vb, ke)

    units = [(h, i) for i in order for h in range(heads)]
    pending = []
    for h, i in units:
        pending.append((h, i) + block(h, i))
        if len(pending) > SCAN_SKEW:
            finish(*pending.pop(0))
    for unit in pending:
        finish(*unit)


def _scan(reverse, batch, lf, q, k, v, acc=None):
    t, width = lf.shape
    rows = SCAN_ROWS
    nc = t // batch // rows
    tri, code = _scan_tables(reverse)
    if reverse:
        idx = lambda b, i: (b * nc + nc - 1 - i, 0)
    else:
        idx = lambda b, i: (b * nc + i, 0)
    row = pl.BlockSpec((rows, width), idx)
    heads = width // HEAD_DIM
    operands = (lf, q, k, v, tri, code) + (() if acc is None else (acc,))
    return pl.pallas_call(
        functools.partial(_scan_kernel, reverse, acc is not None),
        grid=(batch, nc),
        in_specs=[row, row, row, row, _resident(tri.shape), _resident(code.shape)] + ([] if acc is None else [row]),
        out_specs=row,
        out_shape=jax.ShapeDtypeStruct((t, width), F32),
        input_output_aliases={} if acc is None else {len(operands) - 1: 0},
        scratch_shapes=[pltpu.VMEM((heads, HEAD_DIM, HEAD_DIM), F32), pltpu.VMEM((rows, width), F32),
                        pltpu.VMEM((heads * (rows // SCAN_BLOCK) * (SCAN_BLOCK.bit_length() - 1), HEAD_DIM, SCAN_BLOCK),
                                   BF16)],
        compiler_params=pltpu.CompilerParams(dimension_semantics=("arbitrary", "arbitrary"),
                                             vmem_limit_bytes=VMEM_LIMIT),
        name="scan_bwd" if reverse else "scan_fwd",
    )(*operands)


def _mix_out_kernel(tiles_per_seq, x_ref, glu_ref, glu_prev_ref, glu_next_ref, o_ref, og_ref,
                    ga_ref, gb_ref, dww_ref, dwb_ref, lnw_ref, lnb_ref, pw_ref, hnw_ref, ow_ref, wout_ref,
                    xo_ref, ext_ref, shift_ref, act_ref, taps_ref):
    tm, cw = glu_ref.shape
    i = pl.program_id(0)
    first = (i % tiles_per_seq) == 0
    final = (i % tiles_per_seq) == tiles_per_seq - 1

    ext_ref[pl.ds(0, CONV_HALO), :] = jnp.where(first, 0.0, glu_prev_ref[...])
    ext_ref[pl.ds(CONV_HALO, tm), :] = glu_ref[...]
    ext_ref[pl.ds(CONV_HALO + tm, CONV_HALO), :] = jnp.where(final, 0.0, glu_next_ref[...])
    ext = ext_ref[...]
    for p in range(1, SUBLANES):
        shift_ref[p - 1] = pltpu.roll(ext, ext.shape[0] - p, 0)
    base = CONV_HALO - CONV_K // 2
    for j in range(CONV_K):
        taps_ref[j] = jnp.broadcast_to(dww_ref[pl.ds(j, 1), :], (SUBLANES, cw))

    def conv_rows(r):
        r0 = r * CONV_ROWS
        acc = jnp.broadcast_to(dwb_ref[...], (CONV_ROWS, cw))
        for j in range(CONV_K):
            whole, p = divmod(base + j, SUBLANES)
            src = ext_ref if p == 0 else shift_ref.at[p - 1]
            tap = jnp.concatenate([taps_ref[j]] * (CONV_ROWS // SUBLANES), 0)
            acc = acc + tap * src[pl.ds(r0 + whole * SUBLANES, CONV_ROWS), :]
        mu = jnp.mean(acc, axis=-1, keepdims=True)
        cen = acc - mu
        var = jnp.mean(cen * cen, axis=-1, keepdims=True)
        act_ref[pl.ds(r0, CONV_ROWS), :] = _silu(
            cen * lax.rsqrt(var + EPS) * lnw_ref[...] + lnb_ref[...]).astype(BF16)

    o = o_ref[...]
    parts = []
    for h in range(o.shape[1] // HEAD_DIM):
        oh = o[:, h * HEAD_DIM:(h + 1) * HEAD_DIM]
        parts.append(oh * lax.rsqrt(jnp.mean(oh * oh, axis=-1, keepdims=True) + EPS))
    of = jnp.concatenate(parts, axis=1) * hnw_ref[...]
    gated_b = gb_ref[...].astype(F32) * _dot((of * og_ref[...].astype(F32)).astype(BF16), ow_ref[...])

    halves = 2
    rows_per_half = tm // halves
    for half in range(halves):
        for r in range(half * rows_per_half // CONV_ROWS, (half + 1) * rows_per_half // CONV_ROWS):
            conv_rows(r)
        rows = pl.ds(half * rows_per_half, rows_per_half)
        a = _dot(act_ref[rows, :], pw_ref[...])
        y = ga_ref[rows, :].astype(F32) * a + gated_b[half * rows_per_half:(half + 1) * rows_per_half]
        xo_ref[rows, :] = x_ref[rows, :] + _dot(y.astype(BF16), wout_ref[...])


def _mix_out(seq, x2, glu, o, og, ga, gb, dw_w, dw_b, ln_w, ln_b, pw_b, hn_w, ow_b, wout_b):
    t, d = x2.shape
    cw = glu.shape[1]
    tm = TOKEN_TILE
    hb = tm // CONV_HALO
    nhalo = t // CONV_HALO
    row = lambda width: pl.BlockSpec((tm, width), lambda i: (i, 0))
    prev = pl.BlockSpec((CONV_HALO, cw), lambda i: (jnp.maximum(i * hb - 1, 0), 0))
    nxt = pl.BlockSpec((CONV_HALO, cw), lambda i: (jnp.minimum((i + 1) * hb, nhalo - 1), 0))
    weights = (dw_w, dw_b, ln_w, ln_b, pw_b, hn_w, ow_b, wout_b)
    return pl.pallas_call(
        functools.partial(_mix_out_kernel, seq // tm),
        grid=(t // tm,),
        in_specs=[row(d), row(cw), prev, nxt] + [row(d)] * 4 + [_resident(w.shape) for w in weights],
        out_specs=row(d),
        out_shape=jax.ShapeDtypeStruct((t, d), F32),
        scratch_shapes=[pltpu.VMEM((tm + 2 * CONV_HALO, cw), F32),
                        pltpu.VMEM((SUBLANES - 1, tm + 2 * CONV_HALO, cw), F32),
                        pltpu.VMEM((tm, cw), BF16),
                        pltpu.VMEM((CONV_K, SUBLANES, cw), F32)],
        compiler_params=pltpu.CompilerParams(dimension_semantics=("arbitrary",),
                                             vmem_limit_bytes=VMEM_LIMIT),
        name="mix_out",
    )(x2, glu, glu, glu, o, og, ga, gb, *weights)


def _ffn_kernel(tiles_per_seq, ff_chunks, apply_final, x_ref, x_prev_ref, x_next_ref, nw_ref, wup_ref,
                dww_ref, dwb_ref, wdn_ref, fnw_ref, xo_ref, hext_ref, gext_ref):
    tm, d = x_ref.shape
    dff = wdn_ref.shape[0]
    i = pl.program_id(0)
    first = (i % tiles_per_seq) == 0
    final = (i % tiles_per_seq) == tiles_per_seq - 1

    def norm(x):
        return (x * lax.rsqrt(jnp.mean(x * x, axis=-1, keepdims=True) + EPS) * nw_ref[...]).astype(BF16)

    x = x_ref[...]
    hext_ref[pl.ds(0, FFN_HALO), :] = norm(x_prev_ref[...])
    hext_ref[pl.ds(FFN_HALO, tm), :] = norm(x)
    hext_ref[pl.ds(FFN_HALO + tm, FFN_HALO), :] = norm(x_next_ref[...])

    rows = lax.broadcasted_iota(jnp.int32, (tm + 2 * FFN_HALO, 1), 0)
    outside = (first & (rows < FFN_HALO)) | (final & (rows >= FFN_HALO + tm))
    base = FFN_HALO - FFN_CONV_K // 2
    hidden = []
    lo = 0
    for fc in ff_chunks:
        gate = _dot(hext_ref[...], wup_ref[:, lo:lo + fc])
        gext_ref[:, lo:lo + fc] = jnp.where(outside, 0.0, gate)
        conv = jnp.broadcast_to(dwb_ref[:, lo:lo + fc], (tm, fc))
        for j in range(FFN_CONV_K):
            conv = conv + dww_ref[pl.ds(j, 1), lo:lo + fc] * gext_ref[pl.ds(base + j, tm), lo:lo + fc]
        val = _dot(hext_ref[pl.ds(FFN_HALO, tm), :], wup_ref[:, dff + lo:dff + lo + fc])
        hidden.append((_silu(conv) * val).astype(BF16))
        lo += fc
    acc = x
    lo = 0
    for fc, u in zip(ff_chunks, hidden):
        acc = acc + _dot(u, wdn_ref[lo:lo + fc, :])
        lo += fc
    if apply_final:
        acc = acc * lax.rsqrt(jnp.mean(acc * acc, axis=-1, keepdims=True) + EPS) * fnw_ref[...]
    xo_ref[...] = acc


def _ffn_chunks(dff):
    tiles = dff // MXU_COLUMNS
    assert tiles * MXU_COLUMNS == dff
    first = (tiles + 1) // 2 * MXU_COLUMNS
    return (first, dff - first)


def _ffn(seq, apply_final, x2, norm_w, wup_b, dw_w, dw_b, wdn_b, final_w):
    t, d = x2.shape
    dff = wdn_b.shape[0]
    tm = FFN_TILE
    ff_chunks = _ffn_chunks(dff)
    hb = tm // FFN_HALO
    nhalo = t // FFN_HALO
    row = pl.BlockSpec((tm, d), lambda i: (i, 0))
    prev = pl.BlockSpec((FFN_HALO, d), lambda i: (jnp.maximum(i * hb - 1, 0), 0))
    nxt = pl.BlockSpec((FFN_HALO, d), lambda i: (jnp.minimum((i + 1) * hb, nhalo - 1), 0))
    weights = (norm_w, wup_b, dw_w, dw_b, wdn_b, final_w)
    return pl.pallas_call(
        functools.partial(_ffn_kernel, seq // tm, ff_chunks, apply_final),
        grid=(t // tm,),
        in_specs=[row, prev, nxt] + [_resident(w.shape) for w in weights],
        out_specs=row,
        out_shape=jax.ShapeDtypeStruct((t, d), F32),
        scratch_shapes=[pltpu.VMEM((tm + 2 * FFN_HALO, d), BF16),
                        pltpu.VMEM((tm + 2 * FFN_HALO, dff), F32)],
        compiler_params=pltpu.CompilerParams(dimension_semantics=("arbitrary",),
                                             vmem_limit_bytes=VMEM_LIMIT),
        name="ffn",
    )(x2, x2, x2, *weights)


def kernel(x, attn_norm_w, w_in, conv_dw_w, conv_dw_b, conv_ln_w, conv_ln_b, conv_pw_w, lb_logits,
           hgrn_norm_w, hgrn_o_w, w_out, ffn_norm_w, ffn_w_up, ffn_dw_w, ffn_dw_b, ffn_w_down, final_norm_w):
    batch, seq, d = x.shape
    depth = w_in.shape[0]
    x2 = x.reshape(batch * seq, d)
    lbl = lb_logits.astype(F32).reshape(depth * 2, -1)
    vec = lambda w: w.reshape(1, -1).astype(F32)
    col = jnp.arange(w_in.shape[2])
    in_scale = jnp.where((col >= 2 * d) & (col < 3 * d), 1.0, 0.5).astype(F32)
    for layer in range(depth):
        glu, q, v, lf_f, k_f, lf_b, k_b, og, ga, gb = _in_proj(
            layer, depth, x2, vec(attn_norm_w[layer]), (w_in[layer] * in_scale).astype(BF16), lbl)
        o = _scan(False, batch, lf_f, q, k_f, v)
        o = _scan(True, batch, lf_b, q, k_b, v, acc=o)
        x2 = _mix_out(seq, x2, glu, o, og, ga, gb,
                      conv_dw_w[layer], vec(conv_dw_b[layer]), vec(conv_ln_w[layer]), vec(conv_ln_b[layer]),
                      conv_pw_w[layer].astype(BF16), vec(hgrn_norm_w[layer]),
                      hgrn_o_w[layer].astype(BF16), w_out[layer].astype(BF16))
        x2 = _ffn(seq, layer == depth - 1, x2, vec(ffn_norm_w[layer]), ffn_w_up[layer].astype(BF16),
                  ffn_dw_w[layer], vec(ffn_dw_b[layer]), ffn_w_down[layer].astype(BF16), vec(final_norm_w))
    return x2.reshape(batch, seq, d)
```
